```python
import math
import jax, jax.numpy as jnp
from jax import lax
import numpy as np

D_MODEL = 2048
BATCH = 2
SEQ = 4096
DEPTH = 2

GRID_W = 64
CTX_LEN = 256
N_MIXERS = 2
N_NA_LAYERS = (DEPTH + N_MIXERS - 1) // N_MIXERS
N_S5_LAYERS = DEPTH // N_MIXERS
N_HEADS = 16
HEAD_DIM = D_MODEL // N_HEADS
WIN_R = 8
WIN_C = 16
Q_BLK_C = 16
K_BLK_C = 2 * WIN_C
SSM_GROUP = 16
N_GROUPS = D_MODEL // SSM_GROUP
SSM_STATE = 64
N_EXPERTS = 16
EXPERT_FF = D_MODEL
CAPACITY_FACTOR = 2
DN_ALPHA = (2 * DEPTH) ** 0.25
DN_BETA = (8 * DEPTH) ** -0.25
LN_EPS = 1e-5
NEG_INF = -1e30

kernel_name = 'hybrid_natten_s5_ecmoe_dit_block'


def layer_norm(x, g, b):
    xf = x.astype(jnp.float32)
    mu = jnp.mean(xf, -1, keepdims=True)
    var = jnp.mean(jnp.square(xf - mu), -1, keepdims=True)
    return ((xf - mu) * lax.rsqrt(var + LN_EPS) * g + b).astype(x.dtype)


def neighborhood_attention(q, k, v, k_ctx, v_ctx, rpb):
    B, L, H, dh = q.shape
    rows = L // GRID_W
    kr = min(WIN_R, rows)
    qg = q.reshape(B, rows, GRID_W, H, dh)
    kg = k.reshape(B, rows, GRID_W, H, dh)
    vg = v.reshape(B, rows, GRID_W, H, dh)
    r = jnp.arange(rows)
    rs = jnp.clip(r - kr // 2, 0, rows - kr)
    row_idx = rs[:, None] + jnp.arange(kr)[None, :]
    dr_idx = row_idx - r[:, None] + (WIN_R - 1)
    scale = HEAD_DIM ** -0.5
    n_loc = kr * K_BLK_C
    outs = []
    for j in range(GRID_W // Q_BLK_C):
        c0 = j * Q_BLK_C
        cs = min(max(c0 - WIN_C // 2, 0), GRID_W - K_BLK_C)
        qcol = c0 + jnp.arange(Q_BLK_C)
        kcol = cs + jnp.arange(K_BLK_C)
        cstart = jnp.clip(qcol - WIN_C // 2, 0, GRID_W - WIN_C)
        valid = (kcol[None] >= cstart[:, None]) & (kcol[None] < cstart[:, None] + WIN_C)
        dc_idx = jnp.clip(kcol[None] - qcol[:, None] + WIN_C - 1, 0, 2 * WIN_C - 2)
        bias = rpb[:, dr_idx[:, None, :, None], dc_idx[None, :, None, :]]
        qb = qg[:, :, c0:c0 + Q_BLK_C]
        kb = kg[:, row_idx, cs:cs + K_BLK_C]
        vb = vg[:, row_idx, cs:cs + K_BLK_C]
        s_loc = jnp.einsum('brqhd,brkwhd->bhrqkw', qb, kb).astype(jnp.float32) * scale
        s_loc = s_loc + bias.astype(jnp.float32)[None]
        s_loc = jnp.where(valid[:, None, :], s_loc, NEG_INF).reshape(B, H, rows, Q_BLK_C, n_loc)
        s_ctx = jnp.einsum('brqhd,bchd->bhrqc', qb, k_ctx).astype(jnp.float32) * scale
        p = jax.nn.softmax(jnp.concatenate([s_loc, s_ctx], axis=-1), axis=-1).astype(v.dtype)
        p_loc = p[..., :n_loc].reshape(B, H, rows, Q_BLK_C, kr, K_BLK_C)
        o = (jnp.einsum('bhrqkw,brkwhd->brqhd', p_loc, vb)
             + jnp.einsum('bhrqc,bchd->brqhd', p[..., n_loc:], v_ctx))
        outs.append(o)
    return jnp.concatenate(outs, axis=2).reshape(B, L, H * dh)


def context_attention(q, k, v):
    s = jnp.einsum('bqhd,bkhd->bhqk', q, k).astype(jnp.float32) * (HEAD_DIM ** -0.5)
    p = jax.nn.softmax(s, axis=-1).astype(v.dtype)
    o = jnp.einsum('bhqk,bkhd->bqhd', p, v)
    return o.reshape(q.shape[0], q.shape[1], D_MODEL)


def na_mixer(h_lat, h_ctx, w_qkv, w_o, rpb, need_ctx):
    B, L, _ = h_lat.shape
    Lc = h_ctx.shape[1]
    ql, kl, vl = jnp.split((h_lat @ w_qkv).reshape(B, L, 3 * N_HEADS, HEAD_DIM), 3, axis=2)
    qc, kc, vc = jnp.split((h_ctx @ w_qkv).reshape(B, Lc, 3 * N_HEADS, HEAD_DIM), 3, axis=2)
    o_lat = neighborhood_attention(ql, kl, vl, kc, vc, rpb) @ w_o
    o_ctx = context_attention(qc, kc, vc) @ w_o if need_ctx else None
    return o_lat, o_ctx


def s5_discretize(lam_re, lam_im, log_step, b_re, b_im):
    lr = jnp.minimum(lam_re.astype(jnp.float32), -1e-4)
    li = lam_im.astype(jnp.float32)
    dt = jnp.exp(log_step.astype(jnp.float32))[:, None]
    mag = jnp.exp(lr * dt)
    ar = mag * jnp.cos(li * dt)
    ai = mag * jnp.sin(li * dt)
    nr = ar - 1.0
    den = lr * lr + li * li
    cr = ((nr * lr + ai * li) / den)[..., None]
    ci = ((ai * lr - nr * li) / den)[..., None]
    br = cr * b_re - ci * b_im
    bi = cr * b_im + ci * b_re
    return ar, ai, br, bi


def _scan_op(e1, e2):
    a1r, a1i, b1r, b1i = e1
    a2r, a2i, b2r, b2i = e2
    return (a2r * a1r - a2i * a1i, a2r * a1i + a2i * a1r,
            a2r * b1r - a2i * b1i + b2r, a2r * b1i + a2i * b1r + b2i)


def s5_scan(ar, ai, bur, bui, h0r, h0i, reverse):
    first = bur.shape[1] - 1 if reverse else 0
    bur = bur.at[:, first].add(ar * h0r - ai * h0i)
    bui = bui.at[:, first].add(ar * h0i + ai * h0r)

    def one(br, bi):
        a_r = jnp.broadcast_to(ar, br.shape)
        a_i = jnp.broadcast_to(ai, br.shape)
        _, _, xr, xi = lax.associative_scan(_scan_op, (a_r, a_i, br, bi), reverse=reverse, axis=0)
        return xr, xi

    return jax.vmap(one)(bur, bui)


def s5_readout(xr, xi, c_re, c_im):
    return jnp.einsum('bngp,ghp->bngh', xr, c_re) - jnp.einsum('bngp,ghp->bngh', xi, c_im)


def s5_mixer(h_lat, h_ctx, lam_re, lam_im, log_step, b_re, b_im, c_re, c_im, d_skip,
             w_val, w_gate, need_ctx):
    B, n, _ = h_lat.shape
    nc = h_ctx.shape[1]
    u_lat = h_lat.reshape(B, n, N_GROUPS, SSM_GROUP)
    u_ctx = h_ctx.reshape(B, nc, N_GROUPS, SSM_GROUP)
    d = d_skip.reshape(N_GROUPS, SSM_GROUP)
    y_lat = d * u_lat
    y_ctx = d * u_ctx if need_ctx else None
    zero = jnp.zeros((B, N_GROUPS, SSM_STATE), jnp.float32)
    for direction in range(2):
        rev = direction == 1
        ar, ai, br, bi = s5_discretize(lam_re[direction], lam_im[direction], log_step[direction],
                                       b_re[direction], b_im[direction])
        cbr = jnp.einsum('bngh,gph->bngp', u_ctx, br)
        cbi = jnp.einsum('bngh,gph->bngp', u_ctx, bi)
        xcr, xci = s5_scan(ar, ai, cbr, cbi, zero, zero, rev)
        last = 0 if rev else nc - 1
        lbr = jnp.einsum('bngh,gph->bngp', u_lat, br)
        lbi = jnp.einsum('bngh,gph->bngp', u_lat, bi)
        xlr, xli = s5_scan(ar, ai, lbr, lbi, xcr[:, last], xci[:, last], rev)
        y_lat = y_lat + s5_readout(xlr, xli, c_re[direction], c_im[direction])
        if need_ctx:
            y_ctx = y_ctx + s5_readout(xcr, xci, c_re[direction], c_im[direction])

    def glu(y):
        z = jax.nn.gelu(y.reshape(B, -1, D_MODEL)).astype(h_lat.dtype)
        return (z @ w_val) * jax.nn.sigmoid(z @ w_gate)

    return glu(y_lat), (glu(y_ctx) if need_ctx else None)


def expert_choice_ffn(h, w_router, w_gate, w_up, w_down):
    B, n, D = h.shape
    cap = CAPACITY_FACTOR * n // N_EXPERTS
    logits = jnp.einsum('bnd,de->ben', h, w_router).astype(jnp.float32)
    aff = jax.nn.softmax(logits, axis=1)
    gate, idx = lax.top_k(aff, cap)
    xe = jax.vmap(lambda hb, ib: hb[ib])(h, idx)
    a = jnp.einsum('becd,edf->becf', xe, w_gate)
    u = jnp.einsum('becd,edf->becf', xe, w_up)
    ye = jnp.einsum('becf,efd->becd', jax.nn.silu(a) * u, w_down) * gate[..., None].astype(h.dtype)
    return jax.vmap(lambda ib, yb: jnp.zeros((n, D), yb.dtype).at[ib.reshape(-1)].add(
        yb.reshape(-1, D)))(idx, ye)


def setup_inputs(seed: int = 0) -> dict:
    key = jax.random.key(seed)
    ks = jax.random.split(key, 28)
    f32 = jnp.float32
    D = D_MODEL

    def nrm(k, shape, std):
        return jax.random.normal(k, shape, f32) * std

    s5_shape = (N_S5_LAYERS, 2, N_GROUPS, SSM_STATE)
    return {
        'x': nrm(ks[0], (BATCH, SEQ, D), 1.0),
        'c': nrm(ks[1], (BATCH, D), 1.0),
        'ctx': nrm(ks[2], (BATCH, CTX_LEN, D), 1.0),
        'c_ctx': nrm(ks[3], (D,), 1.0),
        'w_mod': nrm(ks[4], (DEPTH, D, 6 * D), 0.5 * D ** -0.5),
        'b_mod': nrm(ks[5], (DEPTH, 6 * D), 0.02),
        'ln_g': 1.0 + nrm(ks[6], (DEPTH, 2, D), 0.02),
        'ln_b': nrm(ks[7], (DEPTH, 2, D), 0.02),
        'na_w_qkv': nrm(ks[8], (N_NA_LAYERS, D, 3 * D), D ** -0.5),
        'na_w_o': nrm(ks[9], (N_NA_LAYERS, D, D), DN_BETA * D ** -0.5),
        'na_rpb': nrm(ks[10], (N_NA_LAYERS, N_HEADS, 2 * WIN_R - 1, 2 * WIN_C - 1), 0.02),
        's5_lam_re': -0.5 + nrm(ks[11], s5_shape, 0.01),
        's5_lam_im': math.pi * jnp.arange(SSM_STATE, dtype=f32) + nrm(ks[12], s5_shape, 0.01),
        's5_log_step': jax.random.uniform(ks[13], (N_S5_LAYERS, 2, N_GROUPS), f32,
                                          math.log(1e-3), math.log(1e-1)),
        's5_b_re': nrm(ks[14], s5_shape + (SSM_GROUP,), (2 * SSM_GROUP) ** -0.5),
        's5_b_im': nrm(ks[15], s5_shape + (SSM_GROUP,), (2 * SSM_GROUP) ** -0.5),
        's5_c_re': nrm(ks[16], (N_S5_LAYERS, 2, N_GROUPS, SSM_GROUP, SSM_STATE), SSM_STATE ** -0.5),
        's5_c_im': nrm(ks[17], (N_S5_LAYERS, 2, N_GROUPS, SSM_GROUP, SSM_STATE), SSM_STATE ** -0.5),
        's5_d': nrm(ks[18], (N_S5_LAYERS, D), 1.0),
        's5_w_val': nrm(ks[19], (N_S5_LAYERS, D, D), DN_BETA * D ** -0.5),
        's5_w_gate': nrm(ks[20], (N_S5_LAYERS, D, D), D ** -0.5),
        'moe_w_router': nrm(ks[21], (DEPTH, D, N_EXPERTS), D ** -0.5),
        'moe_w_gate': nrm(ks[22], (DEPTH, N_EXPERTS, D, EXPERT_FF), D ** -0.5),
        'moe_w_up': nrm(ks[23], (DEPTH, N_EXPERTS, D, EXPERT_FF), D ** -0.5),
        'moe_w_down': nrm(ks[24], (DEPTH, N_EXPERTS, EXPERT_FF, D), DN_BETA * EXPERT_FF ** -0.5),
    }


def reference(x, c, ctx, c_ctx, w_mod, b_mod, ln_g, ln_b, na_w_qkv, na_w_o, na_rpb,
              s5_lam_re, s5_lam_im, s5_log_step, s5_b_re, s5_b_im, s5_c_re, s5_c_im, s5_d,
              s5_w_val, s5_w_gate, moe_w_router, moe_w_gate, moe_w_up, moe_w_down):
    x_lat = x
    x_ctx = ctx
    for i in range(DEPTH):
        need_ctx = i < DEPTH - 1
        m_lat = (jax.nn.silu(c) @ w_mod[i] + b_mod[i])[:, None, :]
        m_ctx = (jax.nn.silu(c_ctx) @ w_mod[i] + b_mod[i])[None, None, :]
        sh1, sc1, g1, sh2, sc2, g2 = jnp.split(m_lat, 6, axis=-1)
        csh1, csc1, cg1, csh2, csc2, cg2 = jnp.split(m_ctx, 6, axis=-1)
        h_lat = x_lat * (1.0 + sc1) + sh1
        h_ctx = x_ctx * (1.0 + csc1) + csh1
        j = i // N_MIXERS
        if i % N_MIXERS == 0:
            o_lat, o_ctx = na_mixer(h_lat, h_ctx, na_w_qkv[j], na_w_o[j], na_rpb[j], need_ctx)
        else:
            o_lat, o_ctx = s5_mixer(h_lat, h_ctx, s5_lam_re[j], s5_lam_im[j], s5_log_step[j],
                                    s5_b_re[j], s5_b_im[j], s5_c_re[j], s5_c_im[j], s5_d[j],
                                    s5_w_val[j], s5_w_gate[j], need_ctx)
        x_lat = layer_norm(DN_ALPHA * x_lat + g1 * o_lat, ln_g[i, 0], ln_b[i, 0])
        h2 = x_lat * (1.0 + sc2) + sh2
        f_lat = expert_choice_ffn(h2, moe_w_router[i], moe_w_gate[i], moe_w_up[i], moe_w_down[i])
        x_lat = layer_norm(DN_ALPHA * x_lat + g2 * f_lat, ln_g[i, 1], ln_b[i, 1])
        if need_ctx:
            x_ctx = layer_norm(DN_ALPHA * x_ctx + cg1 * o_ctx, ln_g[i, 0], ln_b[i, 0])
            hc2 = x_ctx * (1.0 + csc2) + csh2
            f_ctx = expert_choice_ffn(hc2, moe_w_router[i], moe_w_gate[i], moe_w_up[i], moe_w_down[i])
            x_ctx = layer_norm(DN_ALPHA * x_ctx + cg2 * f_ctx, ln_g[i, 1], ln_b[i, 1])
    return x_lat
```

```python
import functools
import math

import jax
import jax.numpy as jnp
from jax import lax
from jax.experimental import pallas as pl
from jax.experimental.pallas import tpu as pltpu

f32 = jnp.float32
bf16 = jnp.bfloat16
i32 = jnp.int32

_GRID_W = 64
_WIN_R = 8
_WIN_C = 16
_N_HEADS = 16
_SSM_GROUP = 16
_CAPACITY_FACTOR = 2
_LN_EPS = 1e-5
_NEG_INF = -1e30

_LANES = 128
_SUBLANES = 8
_VMEM_LIMIT_BYTES = 56 * 1024 * 1024

_NT_DIMS = (((1,), (1,)), ((), ()))


def _cparams(sem):
    return pltpu.CompilerParams(dimension_semantics=sem, vmem_limit_bytes=_VMEM_LIMIT_BYTES)


def _mod_body(cb_ref, w_ref, b_ref, o_ref, *, n_rows):
    d, tn = w_ref.shape
    reps = tn // _LANES

    def step(i, accs):
        k0 = pl.multiple_of(i * _SUBLANES, _SUBLANES)
        w = w_ref[pl.ds(k0, _SUBLANES), :]
        out = []
        for r in range(n_rows):
            cv = cb_ref[r, pl.ds(k0, _SUBLANES), :]
            s = cv * jax.nn.sigmoid(cv)
            out.append(accs[r] + jnp.tile(s, (1, reps)) * w)
        return tuple(out)

    init = tuple(jnp.zeros((_SUBLANES, tn), f32) for _ in range(n_rows))
    accs = lax.fori_loop(0, d // _SUBLANES, step, init)
    rows = [jnp.sum(a, axis=0, keepdims=True) + b_ref[...] for a in accs]
    rows.append(jnp.zeros((_SUBLANES - n_rows, tn), f32))
    o_ref[...] = jnp.concatenate(rows, axis=0)


def _modulation(cond, w_mod, b_mod):
    n_rows, d = cond.shape
    depth, _, n6 = w_mod.shape
    tn = min(1024, n6)
    cb = jnp.broadcast_to(cond[:, :, None], (n_rows, d, _LANES))
    return pl.pallas_call(
        functools.partial(_mod_body, n_rows=n_rows),
        grid=(depth, n6 // tn),
        in_specs=[
            pl.BlockSpec((n_rows, d, _LANES), lambda l, j: (0, 0, 0)),
            pl.BlockSpec((None, d, tn), lambda l, j: (l, 0, j)),
            pl.BlockSpec((None, 1, tn), lambda l, j: (l, 0, j)),
        ],
        out_specs=pl.BlockSpec((None, _SUBLANES, tn), lambda l, j: (l, 0, j)),
        out_shape=jax.ShapeDtypeStruct((depth, _SUBLANES, n6), f32),
        compiler_params=_cparams(("arbitrary", "arbitrary")),
        name="modulation",
    )(cb, w_mod, b_mod.reshape(depth, 1, n6))


def _mod_spec(layer, chunk, d, nargs):
    if nargs == 1:
        return pl.BlockSpec((None, _SUBLANES, d), lambda i: (layer, 0, chunk))
    return pl.BlockSpec((None, _SUBLANES, d), lambda i, j: (layer, 0, chunk))


def _proj_body(x_ref, sc_ref, sh_ref, w_ref, o_ref, h_scr, *, tiles_per_row, row0):
    @pl.when(pl.program_id(1) == 0)
    def _():
        r = row0 + pl.program_id(0) // tiles_per_row
        sc = sc_ref[pl.ds(r, 1), :]
        sh = sh_ref[pl.ds(r, 1), :]
        h_scr[...] = (x_ref[...] * (1.0 + sc) + sh).astype(bf16)

    o_ref[...] = jnp.dot(h_scr[...], w_ref[...], preferred_element_type=f32).astype(o_ref.dtype)


def _mod_proj(x, m, layer, w_bf, *, rows_per_mod, row0):
    mtot, d = x.shape
    n = w_bf.shape[1]
    tm = min(1024, rows_per_mod, mtot)
    tn = min(512, n)
    return pl.pallas_call(
        functools.partial(_proj_body, tiles_per_row=rows_per_mod // tm, row0=row0),
        grid=(mtot // tm, n // tn),
        in_specs=[
            pl.BlockSpec((tm, d), lambda i, j: (i, 0)),
            _mod_spec(layer, 1, d, 2),
            _mod_spec(layer, 0, d, 2),
            pl.BlockSpec((d, tn), lambda i, j: (0, j)),
        ],
        out_specs=pl.BlockSpec((tm, tn), lambda i, j: (i, j)),
        out_shape=jax.ShapeDtypeStruct((mtot, n), bf16),
        scratch_shapes=[pltpu.VMEM((tm, d), bf16)],
        compiler_params=_cparams(("arbitrary", "arbitrary")),
        name="mod_proj",
    )(x, m, m, w_bf)


def _attn_body(rpb_ref, q_ref, k_ref, v_ref, qc_ref, kc_ref, vc_ref, o_ref, oc_ref, p_scr, *,
               rows, dh):
    w = _GRID_W
    n_dr = 2 * _WIN_R - 1
    n_dc = 2 * _WIN_C - 1
    h = pl.program_id(0)
    scale = dh ** -0.5

    @pl.when(pl.program_id(1) == 0)
    def _build():
        c = lax.broadcasted_iota(i32, (w, 2 * w), 0)
        lane = lax.broadcasted_iota(i32, (w, 2 * w), 1)
        kc = lane & (w - 1)
        hi = lane >= w
        delta = jnp.clip(kc - c + (_WIN_C - 1), 0, n_dc - 1)
        cstart = jnp.clip(c - _WIN_C // 2, 0, w - _WIN_C)
        valid = (kc >= cstart) & (kc < cstart + _WIN_C)
        base = h * (n_dr * n_dc)

        def dr_body(dr, carry):
            acc = jnp.zeros((w, 2 * w), f32)
            for d in range(n_dc):
                v0 = rpb_ref[base + dr * n_dc + d]
                v1 = rpb_ref[base + (dr + 1) * n_dc + d]
                acc = jnp.where(delta == d, jnp.where(hi, v1, v0), acc)
            p_scr[dr] = jnp.where(valid, acc, _NEG_INF)
            return carry

        lax.fori_loop(0, n_dr - 1, dr_body, 0)

    kcx = kc_ref[...]
    vcx = vc_ref[...]

    def row_body(r, carry):
        rs = jnp.clip(r - _WIN_R // 2, 0, rows - _WIN_R)
        s0 = rs - r + (_WIN_R - 1)
        q0 = pl.multiple_of(r * w, w)
        k0 = pl.multiple_of(rs * w, w)
        q = q_ref[pl.ds(q0, w), :]
        kw = k_ref[pl.ds(k0, _WIN_R * w), :]
        vw = v_ref[pl.ds(k0, _WIN_R * w), :]
        bias = jnp.concatenate([p_scr[s0 + 2 * j] for j in range(_WIN_R // 2)], axis=1)
        s_loc = lax.dot_general(q, kw, _NT_DIMS, preferred_element_type=f32) * scale + bias
        s_ctx = lax.dot_general(q, kcx, _NT_DIMS, preferred_element_type=f32) * scale
        mx = jnp.maximum(jnp.max(s_loc, axis=1, keepdims=True),
                         jnp.max(s_ctx, axis=1, keepdims=True))
        p_loc = jnp.exp(s_loc - mx)
        p_ctx = jnp.exp(s_ctx - mx)
        den = jnp.sum(p_loc, axis=1, keepdims=True) + jnp.sum(p_ctx, axis=1, keepdims=True)
        o = (jnp.dot(p_loc.astype(bf16), vw, preferred_element_type=f32)
             + jnp.dot(p_ctx.astype(bf16), vcx, preferred_element_type=f32))
        o_ref[pl.ds(q0, w), :] = (o / den).astype(o_ref.dtype)
        return carry

    lax.fori_loop(0, rows, row_body, 0)

    s = lax.dot_general(qc_ref[...], kcx, _NT_DIMS, preferred_element_type=f32) * scale
    p = jnp.exp(s - jnp.max(s, axis=1, keepdims=True))
    den = jnp.sum(p, axis=1, keepdims=True)
    oc = jnp.dot(p.astype(bf16), vcx, preferred_element_type=f32)
    oc_ref[...] = (oc / den).astype(oc_ref.dtype)


def _attention(qkv_lat, qkv_ctx, rpb, *, batch, seq, ctx_len, n_heads):
    d = qkv_lat.shape[1] // 3
    dh = d // n_heads
    rows = seq // _GRID_W
    assert dh == _LANES and _GRID_W * 2 == _LANES and rows >= _WIN_R
    n_dr = 2 * _WIN_R - 1
    return pl.pallas_call(
        functools.partial(_attn_body, rows=rows, dh=dh),
        grid=(n_heads, batch),
        in_specs=[
            pl.BlockSpec(memory_space=pltpu.SMEM),
            pl.BlockSpec((seq, dh), lambda h, b: (b, h)),
            pl.BlockSpec((seq, dh), lambda h, b: (b, n_heads + h)),
            pl.BlockSpec((seq, dh), lambda h, b: (b, 2 * n_heads + h)),
            pl.BlockSpec((ctx_len, dh), lambda h, b: (b, h)),
            pl.BlockSpec((ctx_len, dh), lambda h, b: (b, n_heads + h)),
            pl.BlockSpec((ctx_len, dh), lambda h, b: (b, 2 * n_heads + h)),
        ],
        out_specs=[
            pl.BlockSpec((seq, dh), lambda h, b: (b, h)),
            pl.BlockSpec((ctx_len, dh), lambda h, b: (b, h)),
        ],
        out_shape=[
            jax.ShapeDtypeStruct((batch * seq, d), bf16),
            jax.ShapeDtypeStruct((batch * ctx_len, d), bf16),
        ],
        scratch_shapes=[pltpu.VMEM((n_dr - 1, _GRID_W, 2 * _GRID_W), f32)],
        compiler_params=_cparams(("arbitrary", "arbitrary")),
        name="nbr_attention",
    )(rpb.reshape(-1), qkv_lat, qkv_lat, qkv_lat, qkv_ctx, qkv_ctx, qkv_ctx)


def _layer_norm(y, g, b):
    mu = jnp.mean(y, axis=-1, keepdims=True)
    yc = y - mu
    var = jnp.mean(yc * yc, axis=-1, keepdims=True)
    return yc * lax.rsqrt(var + _LN_EPS) * g + b


def _post_body(*refs, glu, alpha, tiles_per_row, row0):
    if glu:
        a_ref, wv_ref, wg_ref = refs[:3]
        rest = refs[3:]
    else:
        a_ref, wv_ref = refs[:2]
        rest = refs[2:]
    x_ref, g1_ref, sc2_ref, sh2_ref, lng_ref, lnb_ref, wr_ref, x1_ref, h2_ref, lg_ref = rest
    a = a_ref[...]
    o = jnp.dot(a, wv_ref[...], preferred_element_type=f32)
    if glu:
        o = o * jax.nn.sigmoid(jnp.dot(a, wg_ref[...], preferred_element_type=f32))
    r = row0 + pl.program_id(0) // tiles_per_row
    g1 = g1_ref[pl.ds(r, 1), :]
    x1 = _layer_norm(alpha * x_ref[...] + g1 * o, lng_ref[...], lnb_ref[...])
    h2 = x1 * (1.0 + sc2_ref[pl.ds(r, 1), :]) + sh2_ref[pl.ds(r, 1), :]
    x1_ref[...] = x1
    h2_ref[...] = h2
    lg_ref[...] = lax.dot_general(wr_ref[...], h2, _NT_DIMS, preferred_element_type=f32,
                                  precision=lax.Precision.HIGHEST)


def _mixer_post(a_bf, weights_bf, x, m, layer, ln_g, ln_b, w_router_t, *, alpha, rows_per_mod,
                row0):
    mtot, d = x.shape
    n_exp = w_router_t.shape[0]
    glu = len(weights_bf) == 2
    tm = min(256, mtot, rows_per_mod)
    const = lambda i: (0, 0)
    in_specs = [pl.BlockSpec((tm, d), lambda i: (i, 0))]
    in_specs += [pl.BlockSpec((d, d), const) for _ in weights_bf]
    in_specs += [
        pl.BlockSpec((tm, d), lambda i: (i, 0)),
        _mod_spec(layer, 2, d, 1),
        _mod_spec(layer, 4, d, 1),
        _mod_spec(layer, 3, d, 1),
        pl.BlockSpec((1, d), const),
        pl.BlockSpec((1, d), const),
        pl.BlockSpec((n_exp, d), const),
    ]
    return pl.pallas_call(
        functools.partial(_post_body, glu=glu, alpha=alpha, tiles_per_row=rows_per_mod // tm,
                          row0=row0),
        grid=(mtot // tm,),
        in_specs=in_specs,
        out_specs=[
            pl.BlockSpec((tm, d), lambda i: (i, 0)),
            pl.BlockSpec((tm, d), lambda i: (i, 0)),
            pl.BlockSpec((n_exp, tm), lambda i: (0, i)),
        ],
        out_shape=[
            jax.ShapeDtypeStruct((mtot, d), f32),
            jax.ShapeDtypeStruct((mtot, d), f32),
            jax.ShapeDtypeStruct((n_exp, mtot), f32),
        ],
        compiler_params=_cparams(("arbitrary",)),
        name="mixer_post_glu" if glu else "mixer_post",
    )(a_bf, *weights_bf, x, m, m, m, ln_g.reshape(1, d), ln_b.reshape(1, d), w_router_t)


def _cumsum_excl(mask01):
    rows, n = mask01.shape
    r = lax.broadcasted_iota(i32, (_LANES, _LANES), 0)
    c = lax.broadcasted_iota(i32, (_LANES, _LANES), 1)
    tri = jnp.where(r < c, 1.0, 0.0).astype(bf16)
    carry = jnp.zeros((rows, 1), f32)
    outs = []
    for t in range(n // _LANES):
        blk = mask01[:, t * _LANES:(t + 1) * _LANES]
        outs.append(jnp.dot(blk.astype(bf16), tri, preferred_element_type=f32) + carry)
        carry = carry + jnp.sum(blk, axis=1, keepdims=True)
    return outs[0] if len(outs) == 1 else jnp.concatenate(outs, axis=1)


def _route_body(lg_ref, idx_ref, gate_ref, posm_scr, aff_scr, *, n, cap, chunk):
    n_exp = lg_ref.shape[0]
    x = lg_ref[...]
    ex = jnp.exp(x - jnp.max(x, axis=0, keepdims=True))
    aff = ex / jnp.sum(ex, axis=0, keepdims=True)
    bits = pltpu.bitcast(aff, i32)

    thr = jnp.zeros((n_exp, 1), i32)
    for bit in range(30, -1, -1):
        cand = thr | (1 << bit)
        cnt = jnp.sum(jnp.where(bits >= cand, 1.0, 0.0), axis=1, keepdims=True)
        thr = jnp.where(cnt >= cap, cand, thr)
    gt = bits > thr
    eq = bits == thr
    need = cap - jnp.sum(jnp.where(gt, 1.0, 0.0), axis=1, keepdims=True)
    rank_eq = _cumsum_excl(jnp.where(eq, 1.0, 0.0))
    sel = gt | (eq & (rank_eq < need))
    pos = _cumsum_excl(jnp.where(sel, 1.0, 0.0))
    posm_scr[...] = jnp.where(sel, pos, -1.0)
    aff_scr[...] = aff
    tok = lax.broadcasted_iota(i32, (chunk, n), 1).astype(f32)
    tok_base = pl.program_id(0) * n

    def per_expert(e, carry):
        prow = posm_scr[pl.ds(e, 1), :]
        arow = aff_scr[pl.ds(e, 1), :]
        for pc in range(cap // chunk):
            slot = (lax.broadcasted_iota(i32, (chunk, n), 0) + pc * chunk).astype(f32)
            hit = prow == slot
            idx = jnp.sum(jnp.where(hit, tok, 0.0), axis=1, keepdims=True)
            gate = jnp.sum(jnp.where(hit, arow, 0.0), axis=1, keepdims=True)
            idx_ref[e, pl.ds(pc * chunk, chunk), :] = idx.astype(i32) + tok_base
            gate_ref[e, pl.ds(pc * chunk, chunk), :] = gate
        return carry

    lax.fori_loop(0, n_exp, per_expert, 0)


def _route(logits_t, *, batch, n):
    n_exp = logits_t.shape[0]
    cap = _CAPACITY_FACTOR * n // n_exp
    chunk = min(_LANES, cap)
    assert n % _LANES == 0 and cap % chunk == 0
    idx, gate = pl.pallas_call(
        functools.partial(_route_body, n=n, cap=cap, chunk=chunk),
        grid=(batch,),
        in_specs=[pl.BlockSpec((n_exp, n), lambda b: (0, b))],
        out_specs=[
            pl.BlockSpec((None, n_exp, cap, 1), lambda b: (b, 0, 0, 0)),
            pl.BlockSpec((None, n_exp, cap, 1), lambda b: (b, 0, 0, 0)),
        ],
        out_shape=[
            jax.ShapeDtypeStruct((batch, n_exp, cap, 1), i32),
            jax.ShapeDtypeStruct((batch, n_exp, cap, 1), f32),
        ],
        scratch_shapes=[pltpu.VMEM((n_exp, n), f32), pltpu.VMEM((n_exp, n), f32)],
        compiler_params=_cparams(("arbitrary",)),
        name="route_topk",
    )(logits_t)
    return idx[..., 0], gate[..., 0]


def _moe_body(idx_ref, *refs, seg_counts, m_slots):
    n_src = len(seg_counts)
    srcs = refs[:n_src]
    gate_ref, wg_ref, wu_ref, wd_ref, o_ref, xe32, xe16, sem = refs[n_src:]
    e = pl.program_id(0)
    f = pl.program_id(1)

    def row_copy(src, row, slot):
        return pltpu.make_async_copy(src.at[pl.ds(row, 1), :], xe32.at[pl.ds(slot, 1), :], sem)

    @pl.when(f == 0)
    def _gather():
        slot0 = 0
        for src, cnt in zip(srcs, seg_counts):
            def issue(j, c, src=src, slot0=slot0):
                row_copy(src, idx_ref[e * m_slots + slot0 + j], slot0 + j).start()
                return c

            lax.fori_loop(0, cnt, issue, 0)
            slot0 += cnt
        slot0 = 0
        for src, cnt in zip(srcs, seg_counts):
            def wait(j, c, src=src, slot0=slot0):
                row_copy(src, 0, slot0 + j).wait()
                return c

            lax.fori_loop(0, cnt, wait, 0)
            slot0 += cnt
        xe16[...] = xe32[...].astype(bf16)

    x = xe16[...]
    a = jnp.dot(x, wg_ref[...].astype(bf16), preferred_element_type=f32)
    u = jnp.dot(x, wu_ref[...].astype(bf16), preferred_element_type=f32)
    hmid = (a * jax.nn.sigmoid(a) * u).astype(bf16)
    y = jnp.dot(hmid, wd_ref[...].astype(bf16), preferred_element_type=f32)

    @pl.when(f == 0)
    def _():
        o_ref[...] = y

    @pl.when(f > 0)
    def _():
        o_ref[...] += y

    @pl.when(f == pl.num_programs(1) - 1)
    def _():
        o_ref[...] = o_ref[...] * gate_ref[...]


def _moe_experts(idx_flat, srcs, seg_counts, gates, w_gate, w_up, w_down):
    n_exp, d, ff = w_gate.shape
    m_slots = sum(seg_counts)
    tf = min(256, ff)
    n_src = len(srcs)
    grid_spec = pltpu.PrefetchScalarGridSpec(
        num_scalar_prefetch=1,
        grid=(n_exp, ff // tf),
        in_specs=[pl.BlockSpec(memory_space=pl.ANY)] * n_src + [
            pl.BlockSpec((None, m_slots, 1), lambda e, f, idx: (e, 0, 0)),
            pl.BlockSpec((None, d, tf), lambda e, f, idx: (e, 0, f)),
            pl.BlockSpec((None, d, tf), lambda e, f, idx: (e, 0, f)),
            pl.BlockSpec((None, tf, d), lambda e, f, idx: (e, f, 0)),
        ],
        out_specs=pl.BlockSpec((None, m_slots, d), lambda e, f, idx: (e, 0, 0)),
        scratch_shapes=[
            pltpu.VMEM((m_slots, d), f32),
            pltpu.VMEM((m_slots, d), bf16),
            pltpu.SemaphoreType.DMA(()),
        ],
    )
    return pl.pallas_call(
        functools.partial(_moe_body, seg_counts=tuple(seg_counts), m_slots=m_slots),
        grid_spec=grid_spec,
        out_shape=jax.ShapeDtypeStruct((n_exp, m_slots, d), f32),
        compiler_params=_cparams(("arbitrary", "arbitrary")),
        name="moe_experts",
    )(idx_flat, *srcs, gates, w_gate, w_up, w_down)


_SCATTER_UNROLL = 4


def _combine_body(idx_ref, ye_ref, o_ref, *, m_slots, slot_base, cap, n):
    b = pl.program_id(0)
    e = pl.program_id(2)

    @pl.when(e == 0)
    def _():
        o_ref[...] = jnp.zeros_like(o_ref)

    base = e * m_slots + slot_base + b * cap
    tok0 = b * n

    def body(i, carry):
        j0 = i * _SCATTER_UNROLL
        toks = [idx_ref[base + j0 + u] - tok0 for u in range(_SCATTER_UNROLL)]
        vals = [o_ref[pl.ds(toks[u], 1), :] + ye_ref[pl.ds(j0 + u, 1), :]
                for u in range(_SCATTER_UNROLL)]
        for u in range(_SCATTER_UNROLL):
            o_ref[pl.ds(toks[u], 1), :] = vals[u]
        return carry

    lax.fori_loop(0, cap // _SCATTER_UNROLL, body, 0)


def _combine(idx_flat, ye, *, batch, n, cap, slot_base):
    n_exp, m_slots, d = ye.shape
    td = min(512, d)
    assert slot_base % cap == 0 and cap % _SCATTER_UNROLL == 0
    blk0 = slot_base // cap
    grid_spec = pltpu.PrefetchScalarGridSpec(
        num_scalar_prefetch=1,
        grid=(batch, d // td, n_exp),
        in_specs=[pl.BlockSpec((None, cap, td), lambda b, j, e, idx: (e, blk0 + b, j))],
        out_specs=pl.BlockSpec((n, td), lambda b, j, e, idx: (b, j)),
    )
    return pl.pallas_call(
        functools.partial(_combine_body, m_slots=m_slots, slot_base=slot_base, cap=cap, n=n),
        grid_spec=grid_spec,
        out_shape=jax.ShapeDtypeStruct((batch * n, d), f32),
        compiler_params=_cparams(("arbitrary", "arbitrary", "arbitrary")),
        name="moe_combine",
    )(idx_flat, ye)


def _ffn_post_body(x_ref, f_ref, g2_ref, lng_ref, lnb_ref, o_ref, *, alpha, tiles_per_row, row0):
    r = row0 + pl.program_id(0) // tiles_per_row
    g2 = g2_ref[pl.ds(r, 1), :]
    o_ref[...] = _layer_norm(alpha * x_ref[...] + g2 * f_ref[...], lng_ref[...], lnb_ref[...])


def _ffn_post(x1, fo, m, layer, ln_g, ln_b, *, alpha, rows_per_mod, row0):
    mtot, d = x1.shape
    tm = min(512, mtot, rows_per_mod)
    const = lambda i: (0, 0)
    return pl.pallas_call(
        functools.partial(_ffn_post_body, alpha=alpha, tiles_per_row=rows_per_mod // tm, row0=row0),
        grid=(mtot // tm,),
        in_specs=[
            pl.BlockSpec((tm, d), lambda i: (i, 0)),
            pl.BlockSpec((tm, d), lambda i: (i, 0)),
            _mod_spec(layer, 5, d, 1),
            pl.BlockSpec((1, d), const),
            pl.BlockSpec((1, d), const),
        ],
        out_specs=pl.BlockSpec((tm, d), lambda i: (i, 0)),
        out_shape=jax.ShapeDtypeStruct((mtot, d), f32),
        compiler_params=_cparams(("arbitrary",)),
        name="ffn_post",
    )(x1, fo, m, ln_g.reshape(1, d), ln_b.reshape(1, d))


def _disc_body(lr_ref, li_ref, ls_ref, bre_ref, bim_ref, ar_ref, ai_ref, br_ref, bi_ref):
    lr = jnp.minimum(lr_ref[...], -1e-4)
    li = li_ref[...]
    dt = jnp.exp(ls_ref[...])
    mag = jnp.exp(lr * dt)
    ar = mag * jnp.cos(li * dt)
    ai = mag * jnp.sin(li * dt)
    nr = ar - 1.0
    den = lr * lr + li * li
    cr = (nr * lr + ai * li) / den
    ci = (ai * lr - nr * li) / den
    ar_ref[...] = ar
    ai_ref[...] = ai
    br_ref[...] = cr * bre_ref[...] - ci * bim_ref[...]
    bi_ref[...] = cr * bim_ref[...] + ci * bre_ref[...]


def _s5_discretize(lam_re, lam_im, log_step, b_re, b_im):
    shape = b_re.shape
    flat = (shape[0] * shape[1], shape[2] * shape[3])
    bc = lambda a: jnp.broadcast_to(a, shape).reshape(flat)
    outs = pl.pallas_call(
        _disc_body,
        out_shape=[jax.ShapeDtypeStruct(flat, f32)] * 4,
        name="s5_discretize",
    )(bc(lam_re[..., None]), bc(lam_im[..., None]), bc(log_step[:, :, None, None]),
      b_re.reshape(flat), b_im.reshape(flat))
    ar, ai, br, bi = [o.reshape(shape) for o in outs]
    return ar[..., 0], ai[..., 0], br, bi


_SCAN_UNROLL = 8


def _cmul(ar, ai, xr, xi):
    return ar * xr - ai * xi, ar * xi + ai * xr


def _cpow(ar, ai, k):
    rr, ri = jnp.ones_like(ar), jnp.zeros_like(ai)
    while k:
        if k & 1:
            rr, ri = _cmul(ar, ai, rr, ri)
        ar, ai = _cmul(ar, ai, ar, ai)
        k >>= 1
    return rr, ri


def _segment_scan(bu_ref, seg_len, s_dim, ar, ai, init, reverse, store):
    ns = _SUBLANES
    unroll = math.gcd(seg_len, _SCAN_UNROLL)

    def outer(jo, carry):
        xr, xi = carry
        for ji in range(unroll):
            jj = jo * unroll + ji
            j = (seg_len - 1 - jj) if reverse else jj
            r0 = pl.multiple_of(j * ns, ns)
            br = bu_ref[pl.ds(r0, ns), 0:s_dim]
            bi = bu_ref[pl.ds(r0, ns), s_dim:2 * s_dim]
            xr, xi = ar * xr - ai * xi + br, ar * xi + ai * xr + bi
            if store:
                bu_ref[pl.ds(r0, ns), 0:s_dim] = xr
                bu_ref[pl.ds(r0, ns), s_dim:2 * s_dim] = xi
        return xr, xi

    return lax.fori_loop(0, seg_len // unroll, outer, init)


def _segment_inits(ends, a_len, h0, reverse):
    er, ei = ends
    alr, ali = a_len
    ns = _SUBLANES
    order = range(ns - 1, -1, -1) if reverse else range(ns)
    cr, ci = h0
    inits_r, inits_i = [None] * ns, [None] * ns
    for s in order:
        inits_r[s], inits_i[s] = cr, ci
        pr, pi = _cmul(alr, ali, cr, ci)
        cr, ci = pr + er[s:s + 1, :], pi + ei[s:s + 1, :]
    return (jnp.concatenate(inits_r, axis=0), jnp.concatenate(inits_i, axis=0)), (cr, ci)


def _s5_body(x_ref, xc_ref, sh_ref, sc_ref, a_ref, wb_ref, cm_ref, d_ref, z_ref,
             up_scr, ucp_scr, bu_scr, buc_scr, y_scr, yn_scr, *, n, nc, batch):
    ns = _SUBLANES
    seg, segc = n // ns, nc // ns
    s_dim = a_ref.shape[-1]
    b = pl.program_id(0)
    sc_l, sh_l = sc_ref[pl.ds(b, 1), :], sh_ref[pl.ds(b, 1), :]
    sc_c, sh_c = sc_ref[batch:batch + 1, :], sh_ref[batch:batch + 1, :]

    for j in range(seg):
        up_scr[j * ns:(j + 1) * ns, :] = x_ref[pl.ds(j, ns, stride=seg), :] * (1.0 + sc_l) + sh_l
    for j in range(segc):
        ucp_scr[j * ns:(j + 1) * ns, :] = xc_ref[pl.ds(j, ns, stride=segc), :] * (1.0 + sc_c) + sh_c

    u = up_scr[...]
    y_scr[...] = d_ref[...] * u
    u_bf = u.astype(bf16)
    uc_bf = ucp_scr[...].astype(bf16)
    zero = (jnp.zeros((ns, s_dim), f32), jnp.zeros((ns, s_dim), f32))
    for direction in range(2):
        reverse = direction == 1
        ar1, ai1 = a_ref[direction, 0:1, :], a_ref[direction, 1:2, :]
        ar = jnp.broadcast_to(ar1, (ns, s_dim))
        ai = jnp.broadcast_to(ai1, (ns, s_dim))
        buc_scr[...] = jnp.dot(uc_bf, wb_ref[direction], preferred_element_type=f32)
        ends_c = _segment_scan(buc_scr, segc, s_dim, ar, ai, zero, reverse, store=False)
        zero1 = (jnp.zeros((1, s_dim), f32), jnp.zeros((1, s_dim), f32))
        _, h0 = _segment_inits(ends_c, _cpow(ar1, ai1, segc), zero1, reverse)
        bu_scr[...] = jnp.dot(u_bf, wb_ref[direction], preferred_element_type=f32)
        ends = _segment_scan(bu_scr, seg, s_dim, ar, ai, zero, reverse, store=False)
        inits, _ = _segment_inits(ends, _cpow(ar1, ai1, seg), h0, reverse)
        _segment_scan(bu_scr, seg, s_dim, ar, ai, inits, reverse, store=True)
        y_scr[...] += jnp.dot(bu_scr[...].astype(bf16), cm_ref[direction],
                              preferred_element_type=f32)

    for j in range(seg):
        yn_scr[pl.ds(j, ns, stride=seg), :] = y_scr[j * ns:(j + 1) * ns, :]
    z_ref[...] = jax.nn.gelu(yn_scr[...]).astype(z_ref.dtype)


def _s5_core(x_lat, x_ctx, m, layer, a_blk, wb_blk, cm_blk, d_skip, *, batch, n, nc):
    d = x_lat.shape[1]
    n_blk = d // _LANES
    s_dim = a_blk.shape[-1]
    assert n % (_SUBLANES * _SCAN_UNROLL) == 0 and nc % _SUBLANES == 0
    return pl.pallas_call(
        functools.partial(_s5_body, n=n, nc=nc, batch=batch),
        grid=(batch, n_blk),
        in_specs=[
            pl.BlockSpec((n, _LANES), lambda b, g: (b, g)),
            pl.BlockSpec((nc, _LANES), lambda b, g: (b, g)),
            pl.BlockSpec((None, _SUBLANES, _LANES), lambda b, g: (layer, 0, g)),
            pl.BlockSpec((None, _SUBLANES, _LANES), lambda b, g: (layer, 0, n_blk + g)),
            pl.BlockSpec((2, None, 2, s_dim), lambda b, g: (0, g, 0, 0)),
            pl.BlockSpec((2, None, _LANES, 2 * s_dim), lambda b, g: (0, g, 0, 0)),
            pl.BlockSpec((2, None, 2 * s_dim, _LANES), lambda b, g: (0, g, 0, 0)),
            pl.BlockSpec((1, _LANES), lambda b, g: (0, g)),
        ],
        out_specs=pl.BlockSpec((n, _LANES), lambda b, g: (b, g)),
        out_shape=jax.ShapeDtypeStruct((batch * n, d), bf16),
        scratch_shapes=[
            pltpu.VMEM((n, _LANES), f32),
            pltpu.VMEM((nc, _LANES), f32),
            pltpu.VMEM((n, 2 * s_dim), f32),
            pltpu.VMEM((nc, 2 * s_dim), f32),
            pltpu.VMEM((n, _LANES), f32),
            pltpu.VMEM((n, _LANES), f32),
        ],
        compiler_params=_cparams(("arbitrary", "arbitrary")),
        name="s5_core",
    )(x_lat, x_ctx, m, m, a_blk, wb_blk, cm_blk, d_skip.reshape(1, d))


def _s5_block_params(ar, ai, br, bi, c_re, c_im):
    _, n_groups, p_dim, gh = br.shape
    gpb = _LANES // gh
    n_blk = n_groups // gpb
    s_dim = gpb * p_dim
    eye = jnp.eye(gpb, dtype=f32)
    a_blk = jnp.stack([ar.reshape(2, n_blk, s_dim), ai.reshape(2, n_blk, s_dim)], axis=2)

    def b_mat(bx):
        t = bx.reshape(2, n_blk, gpb, p_dim, gh)
        return jnp.einsum("dngph,gk->dnghkp", t, eye).reshape(2, n_blk, gpb * gh, s_dim)

    def c_mat(cx):
        t = cx.reshape(2, n_blk, gpb, gh, p_dim)
        return jnp.einsum("dnghp,gk->dngpkh", t, eye).reshape(2, n_blk, s_dim, gpb * gh)

    wb = jnp.concatenate([b_mat(br), b_mat(bi)], axis=-1).astype(bf16)
    cm = jnp.concatenate([c_mat(c_re), -c_mat(c_im)], axis=-2).astype(bf16)
    return a_blk, wb, cm


def _moe_layer(h2_list, logits_list, dims, w_gate, w_up, w_down, *, batch):
    idxs, gates, caps = [], [], []
    for lg, n in zip(logits_list, dims):
        idx, gate = _route(lg, batch=batch, n=n)
        idxs.append(idx)
        gates.append(gate)
        caps.append(idx.shape[-1])
    n_exp = w_gate.shape[0]
    idx_all = jnp.concatenate([jnp.swapaxes(i, 0, 1).reshape(n_exp, -1) for i in idxs], axis=1)
    gate_all = jnp.concatenate([jnp.swapaxes(g, 0, 1).reshape(n_exp, -1) for g in gates], axis=1)
    seg_counts = [batch * c for c in caps]
    idx_flat = idx_all.reshape(-1)
    ye = _moe_experts(idx_flat, h2_list, seg_counts, gate_all[..., None], w_gate, w_up, w_down)
    outs, slot_base = [], 0
    for n, cap in zip(dims, caps):
        outs.append(_combine(idx_flat, ye, batch=batch, n=n, cap=cap, slot_base=slot_base))
        slot_base += batch * cap
    return outs


def _forward(x, c, ctx, c_ctx, w_mod, b_mod, ln_g, ln_b, na_w_qkv, na_w_o, na_rpb, s5_lam_re,
             s5_lam_im, s5_log_step, s5_b_re, s5_b_im, s5_c_re, s5_c_im, s5_d, s5_w_val, s5_w_gate,
             moe_w_router, moe_w_gate, moe_w_up, moe_w_down, *, n_heads):
    batch, seq, d = x.shape
    ctx_len = ctx.shape[1]
    depth = w_mod.shape[0]
    assert depth == 2
    alpha = (2 * depth) ** 0.25

    cond = jnp.concatenate([c, c_ctx[None, :]], axis=0)
    m = _modulation(cond, w_mod, b_mod)
    x_lat = x.reshape(batch * seq, d)
    x_ctx = ctx.reshape(batch * ctx_len, d)
    lat = dict(rows_per_mod=seq, row0=0)
    cx = dict(rows_per_mod=batch * ctx_len, row0=batch)

    w_qkv = na_w_qkv[0].astype(bf16)
    w_o = na_w_o[0].astype(bf16)
    qkv_lat = _mod_proj(x_lat, m, 0, w_qkv, **lat)
    qkv_ctx = _mod_proj(x_ctx, m, 0, w_qkv, **cx)
    o_lat, o_ctx = _attention(qkv_lat, qkv_ctx, na_rpb[0], batch=batch, seq=seq, ctx_len=ctx_len,
                              n_heads=n_heads)
    wr_t = moe_w_router[0].T
    x1_lat, h2_lat, lg_lat = _mixer_post(o_lat, [w_o], x_lat, m, 0, ln_g[0, 0], ln_b[0, 0], wr_t,
                                         alpha=alpha, **lat)
    x1_ctx, h2_ctx, lg_ctx = _mixer_post(o_ctx, [w_o], x_ctx, m, 0, ln_g[0, 0], ln_b[0, 0], wr_t,
                                         alpha=alpha, **cx)
    f_lat, f_ctx = _moe_layer([h2_lat, h2_ctx], [lg_lat, lg_ctx], [seq, ctx_len], moe_w_gate[0],
                              moe_w_up[0], moe_w_down[0], batch=batch)
    x_lat = _ffn_post(x1_lat, f_lat, m, 0, ln_g[0, 1], ln_b[0, 1], alpha=alpha, **lat)
    x_ctx = _ffn_post(x1_ctx, f_ctx, m, 0, ln_g[0, 1], ln_b[0, 1], alpha=alpha, **cx)

    ar, ai, br, bi = _s5_discretize(s5_lam_re[0], s5_lam_im[0], s5_log_step[0], s5_b_re[0],
                                    s5_b_im[0])
    a_blk, wb_blk, cm_blk = _s5_block_params(ar, ai, br, bi, s5_c_re[0], s5_c_im[0])
    z = _s5_core(x_lat, x_ctx, m, 1, a_blk, wb_blk, cm_blk, s5_d[0], batch=batch, n=seq,
                 nc=ctx_len)
    x1_lat, h2_lat, lg_lat = _mixer_post(z, [s5_w_val[0].astype(bf16), s5_w_gate[0].astype(bf16)],
                                         x_lat, m, 1, ln_g[1, 0], ln_b[1, 0], moe_w_router[1].T,
                                         alpha=alpha, **lat)
    (f_lat,) = _moe_layer([h2_lat], [lg_lat], [seq], moe_w_gate[1], moe_w_up[1], moe_w_down[1],
                          batch=batch)
    x_lat = _ffn_post(x1_lat, f_lat, m, 1, ln_g[1, 1], ln_b[1, 1], alpha=alpha, **lat)
    return x_lat.reshape(batch, seq, d)


def kernel(x, c, ctx, c_ctx, w_mod, b_mod, ln_g, ln_b, na_w_qkv, na_w_o, na_rpb, s5_lam_re,
           s5_lam_im, s5_log_step, s5_b_re, s5_b_im, s5_c_re, s5_c_im, s5_d, s5_w_val, s5_w_gate,
           moe_w_router, moe_w_gate, moe_w_up, moe_w_down):
    return _forward(x, c, ctx, c_ctx, w_mod, b_mod, ln_g, ln_b, na_w_qkv, na_w_o, na_rpb,
                    s5_lam_re, s5_lam_im, s5_log_step, s5_b_re, s5_b_im, s5_c_re, s5_c_im, s5_d,
                    s5_w_val, s5_w_gate, moe_w_router, moe_w_gate, moe_w_up, moe_w_down,
                    n_heads=_N_HEADS)
```

```python
import functools
import math

import jax
import jax.numpy as jnp
from jax import lax
from jax.experimental import pallas as pl
from jax.experimental.pallas import tpu as pltpu

f32 = jnp.float32
bf16 = jnp.bfloat16
i32 = jnp.int32

_GRID_W = 64
_WIN_R = 8
_WIN_C = 16
_N_HEADS = 16
_SSM_GROUP = 16
_CAPACITY_FACTOR = 2
_LN_EPS = 1e-5
_NEG_INF = -1e30

_LANES = 128
_SUBLANES = 8
_VMEM_LIMIT_BYTES = 56 * 1024 * 1024

_NT_DIMS = (((1,), (1,)), ((), ()))


def _cparams(sem):
    return pltpu.CompilerParams(dimension_semantics=sem, vmem_limit_bytes=_VMEM_LIMIT_BYTES)


def _mod_body(cb_ref, w_ref, b_ref, o_ref, *, n_rows):
    d, tn = w_ref.shape
    reps = tn // _LANES

    def step(i, accs):
        k0 = pl.multiple_of(i * _SUBLANES, _SUBLANES)
        w = w_ref[pl.ds(k0, _SUBLANES), :]
        out = []
        for r in range(n_rows):
            cv = cb_ref[r, pl.ds(k0, _SUBLANES), :]
            s = cv * jax.nn.sigmoid(cv)
            out.append(accs[r] + jnp.tile(s, (1, reps)) * w)
        return tuple(out)

    init = tuple(jnp.zeros((_SUBLANES, tn), f32) for _ in range(n_rows))
    accs = lax.fori_loop(0, d // _SUBLANES, step, init)
    rows = [jnp.sum(a, axis=0, keepdims=True) + b_ref[...] for a in accs]
    rows.append(jnp.zeros((_SUBLANES - n_rows, tn), f32))
    o_ref[...] = jnp.concatenate(rows, axis=0)


def _modulation(cond, w_mod, b_mod):
    n_rows, d = cond.shape
    depth, _, n6 = w_mod.shape
    tn = min(1024, n6)
    cb = jnp.broadcast_to(cond[:, :, None], (n_rows, d, _LANES))
    return pl.pallas_call(
        functools.partial(_mod_body, n_rows=n_rows),
        grid=(depth, n6 // tn),
        in_specs=[
            pl.BlockSpec((n_rows, d, _LANES), lambda l, j: (0, 0, 0)),
            pl.BlockSpec((None, d, tn), lambda l, j: (l, 0, j)),
            pl.BlockSpec((None, 1, tn), lambda l, j: (l, 0, j)),
        ],
        out_specs=pl.BlockSpec((None, _SUBLANES, tn), lambda l, j: (l, 0, j)),
        out_shape=jax.ShapeDtypeStruct((depth, _SUBLANES, n6), f32),
        compiler_params=_cparams(("arbitrary", "arbitrary")),
        name="modulation",
    )(cb, w_mod, b_mod.reshape(depth, 1, n6))


def _mod_spec(layer, chunk, d, nargs):
    if nargs == 1:
        return pl.BlockSpec((None, _SUBLANES, d), lambda i: (layer, 0, chunk))
    return pl.BlockSpec((None, _SUBLANES, d), lambda i, j: (layer, 0, chunk))


def _proj_body(x_ref, sc_ref, sh_ref, w_ref, o_ref, h_scr, *, tiles_per_row, row0):
    @pl.when(pl.program_id(1) == 0)
    def _():
        r = row0 + pl.program_id(0) // tiles_per_row
        sc = sc_ref[pl.ds(r, 1), :]
        sh = sh_ref[pl.ds(r, 1), :]
        h_scr[...] = (x_ref[...] * (1.0 + sc) + sh).astype(bf16)

    o_ref[...] = jnp.dot(h_scr[...], w_ref[...], preferred_element_type=f32).astype(o_ref.dtype)


def _mod_proj(x, m, layer, w_bf, *, rows_per_mod, row0):
    mtot, d = x.shape
    n = w_bf.shape[1]
    tm = min(1024, rows_per_mod, mtot)
    tn = min(512, n)
    return pl.pallas_call(
        functools.partial(_proj_body, tiles_per_row=rows_per_mod // tm, row0=row0),
        grid=(mtot // tm, n // tn),
        in_specs=[
            pl.BlockSpec((tm, d), lambda i, j: (i, 0)),
            _mod_spec(layer, 1, d, 2),
            _mod_spec(layer, 0, d, 2),
            pl.BlockSpec((d, tn), lambda i, j: (0, j)),
        ],
        out_specs=pl.BlockSpec((tm, tn), lambda i, j: (i, j)),
        out_shape=jax.ShapeDtypeStruct((mtot, n), bf16),
        scratch_shapes=[pltpu.VMEM((tm, d), bf16)],
        compiler_params=_cparams(("arbitrary", "arbitrary")),
        name="mod_proj",
    )(x, m, m, w_bf)


_ATTN_ROW_UNROLL = 4


def _attn_body(rpb_ref, q_ref, k_ref, v_ref, qc_ref, kc_ref, vc_ref, o_ref, oc_ref, p_scr, *,
               rows, dh):
    w = _GRID_W
    n_dr = 2 * _WIN_R - 1
    n_dc = 2 * _WIN_C - 1
    h = pl.program_id(0)
    scale = dh ** -0.5

    @pl.when(pl.program_id(1) == 0)
    def _build():
        c = lax.broadcasted_iota(i32, (w, 2 * w), 0)
        lane = lax.broadcasted_iota(i32, (w, 2 * w), 1)
        kc = lane & (w - 1)
        hi = lane >= w
        delta = jnp.clip(kc - c + (_WIN_C - 1), 0, n_dc - 1)
        cstart = jnp.clip(c - _WIN_C // 2, 0, w - _WIN_C)
        valid = (kc >= cstart) & (kc < cstart + _WIN_C)
        base = h * (n_dr * n_dc)

        def dr_body(dr, carry):
            acc = jnp.zeros((w, 2 * w), f32)
            for d in range(n_dc):
                v0 = rpb_ref[base + dr * n_dc + d]
                v1 = rpb_ref[base + (dr + 1) * n_dc + d]
                acc = jnp.where(delta == d, jnp.where(hi, v1, v0), acc)
            p_scr[dr] = jnp.where(valid, acc, _NEG_INF)
            return carry

        lax.fori_loop(0, n_dr - 1, dr_body, 0)

    kcx = kc_ref[...]
    vcx = vc_ref[...]

    def one_row(r):
        rs = jnp.clip(r - _WIN_R // 2, 0, rows - _WIN_R)
        s0 = rs - r + (_WIN_R - 1)
        q0 = pl.multiple_of(r * w, w)
        k0 = pl.multiple_of(rs * w, w)
        q = q_ref[pl.ds(q0, w), :]
        kw = k_ref[pl.ds(k0, _WIN_R * w), :]
        vw = v_ref[pl.ds(k0, _WIN_R * w), :]
        bias = jnp.concatenate([p_scr[s0 + 2 * j] for j in range(_WIN_R // 2)], axis=1)
        s_loc = lax.dot_general(q, kw, _NT_DIMS, preferred_element_type=f32) * scale + bias
        s_ctx = lax.dot_general(q, kcx, _NT_DIMS, preferred_element_type=f32) * scale
        mx = jnp.maximum(jnp.max(s_loc, axis=1, keepdims=True),
                         jnp.max(s_ctx, axis=1, keepdims=True))
        p_loc = jnp.exp(s_loc - mx)
        p_ctx = jnp.exp(s_ctx - mx)
        den = jnp.sum(p_loc, axis=1, keepdims=True) + jnp.sum(p_ctx, axis=1, keepdims=True)
        o = (jnp.dot(p_loc.astype(bf16), vw, preferred_element_type=f32)
             + jnp.dot(p_ctx.astype(bf16), vcx, preferred_element_type=f32))
        o_ref[pl.ds(q0, w), :] = (o / den).astype(o_ref.dtype)

    def row_body(i, carry):
        for u in range(_ATTN_ROW_UNROLL):
            one_row(i * _ATTN_ROW_UNROLL + u)
        return carry

    lax.fori_loop(0, rows // _ATTN_ROW_UNROLL, row_body, 0)

    s = lax.dot_general(qc_ref[...], kcx, _NT_DIMS, preferred_element_type=f32) * scale
    p = jnp.exp(s - jnp.max(s, axis=1, keepdims=True))
    den = jnp.sum(p, axis=1, keepdims=True)
    oc = jnp.dot(p.astype(bf16), vcx, preferred_element_type=f32)
    oc_ref[...] = (oc / den).astype(oc_ref.dtype)


def _attention(qkv_lat, qkv_ctx, rpb, *, batch, seq, ctx_len, n_heads):
    d = qkv_lat.shape[1] // 3
    dh = d // n_heads
    rows = seq // _GRID_W
    assert dh == _LANES and _GRID_W * 2 == _LANES and rows >= _WIN_R
    assert rows % _ATTN_ROW_UNROLL == 0
    n_dr = 2 * _WIN_R - 1
    return pl.pallas_call(
        functools.partial(_attn_body, rows=rows, dh=dh),
        grid=(n_heads, batch),
        in_specs=[
            pl.BlockSpec(memory_space=pltpu.SMEM),
            pl.BlockSpec((seq, dh), lambda h, b: (b, h)),
            pl.BlockSpec((seq, dh), lambda h, b: (b, n_heads + h)),
            pl.BlockSpec((seq, dh), lambda h, b: (b, 2 * n_heads + h)),
            pl.BlockSpec((ctx_len, dh), lambda h, b: (b, h)),
            pl.BlockSpec((ctx_len, dh), lambda h, b: (b, n_heads + h)),
            pl.BlockSpec((ctx_len, dh), lambda h, b: (b, 2 * n_heads + h)),
        ],
        out_specs=[
            pl.BlockSpec((seq, dh), lambda h, b: (b, h)),
            pl.BlockSpec((ctx_len, dh), lambda h, b: (b, h)),
        ],
        out_shape=[
            jax.ShapeDtypeStruct((batch * seq, d), bf16),
            jax.ShapeDtypeStruct((batch * ctx_len, d), bf16),
        ],
        scratch_shapes=[pltpu.VMEM((n_dr - 1, _GRID_W, 2 * _GRID_W), f32)],
        compiler_params=_cparams(("arbitrary", "arbitrary")),
        name="nbr_attention",
    )(rpb.reshape(-1), qkv_lat, qkv_lat, qkv_lat, qkv_ctx, qkv_ctx, qkv_ctx)


def _layer_norm(y, g, b):
    mu = jnp.mean(y, axis=-1, keepdims=True)
    yc = y - mu
    var = jnp.mean(yc * yc, axis=-1, keepdims=True)
    return yc * lax.rsqrt(var + _LN_EPS) * g + b


def _post_body(*refs, glu, alpha, tiles_per_row, row0):
    if glu:
        a_ref, wv_ref, wg_ref = refs[:3]
        rest = refs[3:]
    else:
        a_ref, wv_ref = refs[:2]
        rest = refs[2:]
    x_ref, g1_ref, sc2_ref, sh2_ref, lng_ref, lnb_ref, wr_ref, x1_ref, h2_ref, lg_ref = rest
    a = a_ref[...]
    o = jnp.dot(a, wv_ref[...], preferred_element_type=f32)
    if glu:
        o = o * jax.nn.sigmoid(jnp.dot(a, wg_ref[...], preferred_element_type=f32))
    r = row0 + pl.program_id(0) // tiles_per_row
    g1 = g1_ref[pl.ds(r, 1), :]
    x1 = _layer_norm(alpha * x_ref[...] + g1 * o, lng_ref[...], lnb_ref[...])
    h2 = x1 * (1.0 + sc2_ref[pl.ds(r, 1), :]) + sh2_ref[pl.ds(r, 1), :]
    x1_ref[...] = x1
    h2_ref[...] = _pack_bf16_pairs(h2)
    lg_ref[...] = lax.dot_general(wr_ref[...], h2, _NT_DIMS, preferred_element_type=f32,
                                  precision=lax.Precision.HIGHEST)


def _mixer_post(a_bf, weights_bf, x, m, layer, ln_g, ln_b, w_router_t, *, alpha, rows_per_mod,
                row0):
    mtot, d = x.shape
    n_exp = w_router_t.shape[0]
    glu = len(weights_bf) == 2
    tm = min(256 if glu else 512, mtot, rows_per_mod)
    const = lambda i: (0, 0)
    in_specs = [pl.BlockSpec((tm, d), lambda i: (i, 0))]
    in_specs += [pl.BlockSpec((d, d), const, pipeline_mode=pl.Buffered(1)) for _ in weights_bf]
    in_specs += [
        pl.BlockSpec((tm, d), lambda i: (i, 0)),
        _mod_spec(layer, 2, d, 1),
        _mod_spec(layer, 4, d, 1),
        _mod_spec(layer, 3, d, 1),
        pl.BlockSpec((1, d), const),
        pl.BlockSpec((1, d), const),
        pl.BlockSpec((n_exp, d), const),
    ]
    return pl.pallas_call(
        functools.partial(_post_body, glu=glu, alpha=alpha, tiles_per_row=rows_per_mod // tm,
                          row0=row0),
        grid=(mtot // tm,),
        in_specs=in_specs,
        out_specs=[
            pl.BlockSpec((tm, d), lambda i: (i, 0)),
            pl.BlockSpec((tm, d // 2), lambda i: (i, 0)),
            pl.BlockSpec((n_exp, tm), lambda i: (0, i)),
        ],
        out_shape=[
            jax.ShapeDtypeStruct((mtot, d), f32),
            jax.ShapeDtypeStruct((mtot, d // 2), jnp.uint32),
            jax.ShapeDtypeStruct((n_exp, mtot), f32),
        ],
        compiler_params=_cparams(("arbitrary",)),
        name="mixer_post_glu" if glu else "mixer_post",
    )(a_bf, *weights_bf, x, m, m, m, ln_g.reshape(1, d), ln_b.reshape(1, d), w_router_t)


def _cumsum_excl(mask01):
    rows, n = mask01.shape
    r = lax.broadcasted_iota(i32, (_LANES, _LANES), 0)
    c = lax.broadcasted_iota(i32, (_LANES, _LANES), 1)
    tri = jnp.where(r < c, 1.0, 0.0).astype(bf16)
    carry = jnp.zeros((rows, 1), f32)
    outs = []
    for t in range(n // _LANES):
        blk = mask01[:, t * _LANES:(t + 1) * _LANES]
        outs.append(jnp.dot(blk.astype(bf16), tri, preferred_element_type=f32) + carry)
        carry = carry + jnp.sum(blk, axis=1, keepdims=True)
    return outs[0] if len(outs) == 1 else jnp.concatenate(outs, axis=1)


def _route_body(lg_ref, idx_ref, gate_ref, posm_scr, aff_scr, *, n, cap, chunk):
    n_exp = lg_ref.shape[0]
    x = lg_ref[...]
    ex = jnp.exp(x - jnp.max(x, axis=0, keepdims=True))
    aff = ex / jnp.sum(ex, axis=0, keepdims=True)
    bits = pltpu.bitcast(aff, i32)

    thr = jnp.zeros((n_exp, 1), i32)
    for bit in range(30, -1, -1):
        cand = thr | (1 << bit)
        cnt = jnp.sum(jnp.where(bits >= cand, 1.0, 0.0), axis=1, keepdims=True)
        thr = jnp.where(cnt >= cap, cand, thr)
    gt = bits > thr
    eq = bits == thr
    need = cap - jnp.sum(jnp.where(gt, 1.0, 0.0), axis=1, keepdims=True)
    rank_eq = _cumsum_excl(jnp.where(eq, 1.0, 0.0))
    sel = gt | (eq & (rank_eq < need))
    pos = _cumsum_excl(jnp.where(sel, 1.0, 0.0))
    posm_scr[...] = jnp.where(sel, pos, -1.0)
    aff_scr[...] = aff
    tok = lax.broadcasted_iota(i32, (chunk, n), 1).astype(f32)
    tok_base = pl.program_id(0) * n

    def per_expert(e, carry):
        prow = posm_scr[pl.ds(e, 1), :]
        arow = aff_scr[pl.ds(e, 1), :]
        for pc in range(cap // chunk):
            slot = (lax.broadcasted_iota(i32, (chunk, n), 0) + pc * chunk).astype(f32)
            hit = prow == slot
            idx = jnp.sum(jnp.where(hit, tok, 0.0), axis=1, keepdims=True)
            gate = jnp.sum(jnp.where(hit, arow, 0.0), axis=1, keepdims=True)
            idx_ref[e, pl.ds(pc * chunk, chunk), :] = idx.astype(i32) + tok_base
            gate_ref[e, pl.ds(pc * chunk, chunk), :] = gate
        return carry

    lax.fori_loop(0, n_exp, per_expert, 0)


def _route(logits_t, *, batch, n):
    n_exp = logits_t.shape[0]
    cap = _CAPACITY_FACTOR * n // n_exp
    chunk = min(_LANES, cap)
    assert n % _LANES == 0 and cap % chunk == 0
    idx, gate = pl.pallas_call(
        functools.partial(_route_body, n=n, cap=cap, chunk=chunk),
        grid=(batch,),
        in_specs=[pl.BlockSpec((n_exp, n), lambda b: (0, b))],
        out_specs=[
            pl.BlockSpec((None, n_exp, cap, 1), lambda b: (b, 0, 0, 0)),
            pl.BlockSpec((None, n_exp, cap, 1), lambda b: (b, 0, 0, 0)),
        ],
        out_shape=[
            jax.ShapeDtypeStruct((batch, n_exp, cap, 1), i32),
            jax.ShapeDtypeStruct((batch, n_exp, cap, 1), f32),
        ],
        scratch_shapes=[pltpu.VMEM((n_exp, n), f32), pltpu.VMEM((n_exp, n), f32)],
        compiler_params=_cparams(("arbitrary",)),
        name="route_topk",
    )(logits_t)
    return idx[..., 0], gate[..., 0]


_GATHER_UNROLL = 8


def _pack_bf16_pairs(h):
    half = h.shape[1] // 2
    u = pltpu.bitcast(h.astype(bf16).astype(f32), jnp.uint32)
    return (u[:, :half] & jnp.uint32(0xFFFF0000)) | (u[:, half:] >> 16)


def _unpack_bf16_pairs(p):
    hi = pltpu.bitcast(p & jnp.uint32(0xFFFF0000), f32).astype(bf16)
    lo = pltpu.bitcast(p << 16, f32).astype(bf16)
    return jnp.concatenate([hi, lo], axis=1)


def _moe_body(idx_ref, *refs, seg_counts, m_slots, n_tiles):
    n_src = len(seg_counts)
    srcs = refs[:n_src]
    gate_ref, wg_ref, wu_ref, wd_ref, o_ref, xp_scr, xe16, hmid, sem = refs[n_src:]
    e = pl.program_id(0)
    j = pl.program_id(1)
    tf = wg_ref.shape[1]

    def row_copy(src, row, slot):
        return pltpu.make_async_copy(src.at[pl.ds(row, 1), :], xp_scr.at[pl.ds(slot, 1), :], sem)

    def for_each_slot(fn):
        slot0 = 0
        for src, cnt in zip(srcs, seg_counts):
            def body(i, c, src=src, slot0=slot0):
                for u in range(_GATHER_UNROLL):
                    fn(src, slot0 + i * _GATHER_UNROLL + u)
                return c

            lax.fori_loop(0, cnt // _GATHER_UNROLL, body, 0)
            slot0 += cnt

    def issue_gather(expert):
        for_each_slot(lambda src, s: row_copy(src, idx_ref[expert * m_slots + s], s).start())

    @pl.when(j == 0)
    def _rows():
        @pl.when(e == 0)
        def _():
            issue_gather(0)

        for_each_slot(lambda src, s: row_copy(src, 0, s).wait())
        xe16[...] = _unpack_bf16_pairs(xp_scr[...])

        @pl.when(e + 1 < pl.num_programs(0))
        def _():
            issue_gather(e + 1)

    @pl.when(j < n_tiles)
    def _up():
        x = xe16[...]
        a = jnp.dot(x, wg_ref[...].astype(bf16), preferred_element_type=f32)
        u = jnp.dot(x, wu_ref[...].astype(bf16), preferred_element_type=f32)
        hmid[j] = (a * jax.nn.sigmoid(a) * u).astype(bf16)

    @pl.when(j >= n_tiles)
    def _down():
        y = jnp.dot(hmid[0], wd_ref[0:tf, :].astype(bf16), preferred_element_type=f32)
        for k in range(1, n_tiles):
            y += jnp.dot(hmid[k], wd_ref[k * tf:(k + 1) * tf, :].astype(bf16),
                         preferred_element_type=f32)
        o_ref[...] = y * gate_ref[...]


def _moe_experts(idx_flat, srcs, seg_counts, gates, w_gate, w_up, w_down, layer):
    _, n_exp, d, ff = w_gate.shape
    m_slots = sum(seg_counts)
    tf = min(512, ff)
    n_tiles = ff // tf
    assert d // tf == n_tiles and all(c % _GATHER_UNROLL == 0 for c in seg_counts)
    n_src = len(srcs)
    up_idx = lambda e, j, idx: (layer, e, 0, jnp.minimum(j, n_tiles - 1))
    dn_idx = lambda e, j, idx: (layer, e, 0, jnp.maximum(j - n_tiles, 0))
    grid_spec = pltpu.PrefetchScalarGridSpec(
        num_scalar_prefetch=1,
        grid=(n_exp, 2 * n_tiles),
        in_specs=[pl.BlockSpec(memory_space=pl.ANY)] * n_src + [
            pl.BlockSpec((None, m_slots, 1), lambda e, j, idx: (e, 0, 0)),
            pl.BlockSpec((None, None, d, tf), up_idx),
            pl.BlockSpec((None, None, d, tf), up_idx),
            pl.BlockSpec((None, None, ff, tf), dn_idx),
        ],
        out_specs=pl.BlockSpec((None, m_slots, tf),
                               lambda e, j, idx: (e, 0, jnp.maximum(j - n_tiles, 0))),
        scratch_shapes=[
            pltpu.VMEM((m_slots, d // 2), jnp.uint32),
            pltpu.VMEM((m_slots, d), bf16),
            pltpu.VMEM((n_tiles, m_slots, tf), bf16),
            pltpu.SemaphoreType.DMA(()),
        ],
    )
    return pl.pallas_call(
        functools.partial(_moe_body, seg_counts=tuple(seg_counts), m_slots=m_slots,
                          n_tiles=n_tiles),
        grid_spec=grid_spec,
        out_shape=jax.ShapeDtypeStruct((n_exp, m_slots, d), f32),
        compiler_params=_cparams(("arbitrary", "arbitrary")),
        name="moe_experts",
    )(idx_flat, *srcs, gates, w_gate, w_up, w_down)


_SCATTER_UNROLL = 2


def _combine_body(idx_ref, ye_ref, o_hbm, acc, sem, *, m_slots, slot_base, cap, n):
    b = pl.program_id(0)
    e = pl.program_id(1)

    @pl.when(e == 0)
    def _():
        acc[...] = jnp.zeros_like(acc)

    base = e * m_slots + slot_base + b * cap
    tok0 = b * n

    def body(i, carry):
        j0 = i * _SCATTER_UNROLL
        toks = [idx_ref[base + j0 + u] - tok0 for u in range(_SCATTER_UNROLL)]
        vals = [acc[pl.ds(toks[u], 1), :] + ye_ref[pl.ds(j0 + u, 1), :]
                for u in range(_SCATTER_UNROLL)]
        for u in range(_SCATTER_UNROLL):
            acc[pl.ds(toks[u], 1), :] = vals[u]
        return carry

    lax.fori_loop(0, cap // _SCATTER_UNROLL, body, 0)

    @pl.when(e == pl.num_programs(1) - 1)
    def _():
        out = pltpu.make_async_copy(acc, o_hbm.at[pl.ds(pl.multiple_of(b * n, n), n), :], sem)
        out.start()
        out.wait()


def _combine(idx_flat, ye, *, batch, n, cap, slot_base):
    n_exp, m_slots, d = ye.shape
    assert slot_base % cap == 0 and cap % _SCATTER_UNROLL == 0
    blk0 = slot_base // cap
    grid_spec = pltpu.PrefetchScalarGridSpec(
        num_scalar_prefetch=1,
        grid=(batch, n_exp),
        in_specs=[pl.BlockSpec((None, cap, d), lambda b, e, idx: (e, blk0 + b, 0))],
        out_specs=pl.BlockSpec(memory_space=pl.ANY),
        scratch_shapes=[pltpu.VMEM((n, d), f32), pltpu.SemaphoreType.DMA(())],
    )
    return pl.pallas_call(
        functools.partial(_combine_body, m_slots=m_slots, slot_base=slot_base, cap=cap, n=n),
        grid_spec=grid_spec,
        out_shape=jax.ShapeDtypeStruct((batch * n, d), f32),
        compiler_params=_cparams(("arbitrary", "arbitrary")),
        name="moe_combine",
    )(idx_flat, ye)


def _ffn_post_body(x_ref, f_ref, g2_ref, lng_ref, lnb_ref, o_ref, *, alpha, tiles_per_row, row0):
    r = row0 + pl.program_id(0) // tiles_per_row
    g2 = g2_ref[pl.ds(r, 1), :]
    o_ref[...] = _layer_norm(alpha * x_ref[...] + g2 * f_ref[...], lng_ref[...], lnb_ref[...])


def _ffn_post(x1, fo, m, layer, ln_g, ln_b, *, alpha, rows_per_mod, row0):
    mtot, d = x1.shape
    tm = min(512, mtot, rows_per_mod)
    const = lambda i: (0, 0)
    return pl.pallas_call(
        functools.partial(_ffn_post_body, alpha=alpha, tiles_per_row=rows_per_mod // tm, row0=row0),
        grid=(mtot // tm,),
        in_specs=[
            pl.BlockSpec((tm, d), lambda i: (i, 0)),
            pl.BlockSpec((tm, d), lambda i: (i, 0)),
            _mod_spec(layer, 5, d, 1),
            pl.BlockSpec((1, d), const),
            pl.BlockSpec((1, d), const),
        ],
        out_specs=pl.BlockSpec((tm, d), lambda i: (i, 0)),
        out_shape=jax.ShapeDtypeStruct((mtot, d), f32),
        compiler_params=_cparams(("arbitrary",)),
        name="ffn_post",
    )(x1, fo, m, ln_g.reshape(1, d), ln_b.reshape(1, d))


def _disc_body(lr_ref, li_ref, ls_ref, bre_ref, bim_ref, ar_ref, ai_ref, br_ref, bi_ref):
    lr = jnp.minimum(lr_ref[...], -1e-4)
    li = li_ref[...]
    dt = jnp.exp(ls_ref[...])
    mag = jnp.exp(lr * dt)
    ar = mag * jnp.cos(li * dt)
    ai = mag * jnp.sin(li * dt)
    nr = ar - 1.0
    den = lr * lr + li * li
    cr = (nr * lr + ai * li) / den
    ci = (ai * lr - nr * li) / den
    ar_ref[...] = ar
    ai_ref[...] = ai
    br_ref[...] = cr * bre_ref[...] - ci * bim_ref[...]
    bi_ref[...] = cr * bim_ref[...] + ci * bre_ref[...]


def _s5_discretize(lam_re, lam_im, log_step, b_re, b_im):
    shape = b_re.shape
    flat = (shape[0] * shape[1], shape[2] * shape[3])
    bc = lambda a: jnp.broadcast_to(a, shape).reshape(flat)
    outs = pl.pallas_call(
        _disc_body,
        out_shape=[jax.ShapeDtypeStruct(flat, f32)] * 4,
        name="s5_discretize",
    )(bc(lam_re[..., None]), bc(lam_im[..., None]), bc(log_step[:, :, None, None]),
      b_re.reshape(flat), b_im.reshape(flat))
    ar, ai, br, bi = [o.reshape(shape) for o in outs]
    return ar[..., 0], ai[..., 0], br, bi


_SCAN_UNROLL = 8


def _cmul(ar, ai, xr, xi):
    return ar * xr - ai * xi, ar * xi + ai * xr


def _cpow(ar, ai, k):
    rr, ri = jnp.ones_like(ar), jnp.zeros_like(ai)
    while k:
        if k & 1:
            rr, ri = _cmul(ar, ai, rr, ri)
        ar, ai = _cmul(ar, ai, ar, ai)
        k >>= 1
    return rr, ri


def _segment_scan(bu_ref, seg_len, s_dim, ar, ai, init, reverse, store):
    ns = _SUBLANES
    unroll = math.gcd(seg_len, _SCAN_UNROLL)

    def outer(jo, carry):
        xr, xi = carry
        for ji in range(unroll):
            jj = jo * unroll + ji
            j = (seg_len - 1 - jj) if reverse else jj
            r0 = pl.multiple_of(j * ns, ns)
            br = bu_ref[pl.ds(r0, ns), 0:s_dim]
            bi = bu_ref[pl.ds(r0, ns), s_dim:2 * s_dim]
            xr, xi = ar * xr - ai * xi + br, ar * xi + ai * xr + bi
            if store:
                bu_ref[pl.ds(r0, ns), 0:s_dim] = xr
                bu_ref[pl.ds(r0, ns), s_dim:2 * s_dim] = xi
        return xr, xi

    return lax.fori_loop(0, seg_len // unroll, outer, init)


def _segment_inits(ends, a_len, h0, reverse):
    er, ei = ends
    alr, ali = a_len
    ns = _SUBLANES
    order = range(ns - 1, -1, -1) if reverse else range(ns)
    cr, ci = h0
    inits_r, inits_i = [None] * ns, [None] * ns
    for s in order:
        inits_r[s], inits_i[s] = cr, ci
        pr, pi = _cmul(alr, ali, cr, ci)
        cr, ci = pr + er[s:s + 1, :], pi + ei[s:s + 1, :]
    return (jnp.concatenate(inits_r, axis=0), jnp.concatenate(inits_i, axis=0)), (cr, ci)


def _s5_body(x_ref, xc_ref, sh_ref, sc_ref, a_ref, wb_ref, cm_ref, d_ref, z_ref,
             up_scr, ucp_scr, bu_scr, buc_scr, y_scr, yn_scr, *, n, nc, batch):
    ns = _SUBLANES
    seg, segc = n // ns, nc // ns
    s_dim = a_ref.shape[-1]
    b = pl.program_id(0)
    sc_l, sh_l = sc_ref[pl.ds(b, 1), :], sh_ref[pl.ds(b, 1), :]
    sc_c, sh_c = sc_ref[batch:batch + 1, :], sh_ref[batch:batch + 1, :]

    for j in range(seg):
        up_scr[j * ns:(j + 1) * ns, :] = x_ref[pl.ds(j, ns, stride=seg), :] * (1.0 + sc_l) + sh_l
    for j in range(segc):
        ucp_scr[j * ns:(j + 1) * ns, :] = xc_ref[pl.ds(j, ns, stride=segc), :] * (1.0 + sc_c) + sh_c

    u = up_scr[...]
    y_scr[...] = d_ref[...] * u
    u_bf = u.astype(bf16)
    uc_bf = ucp_scr[...].astype(bf16)
    zero = (jnp.zeros((ns, s_dim), f32), jnp.zeros((ns, s_dim), f32))
    for direction in range(2):
        reverse = direction == 1
        ar1, ai1 = a_ref[direction, 0:1, :], a_ref[direction, 1:2, :]
        ar = jnp.broadcast_to(ar1, (ns, s_dim))
        ai = jnp.broadcast_to(ai1, (ns, s_dim))
        buc_scr[...] = jnp.dot(uc_bf, wb_ref[direction], preferred_element_type=f32)
        ends_c = _segment_scan(buc_scr, segc, s_dim, ar, ai, zero, reverse, store=False)
        zero1 = (jnp.zeros((1, s_dim), f32), jnp.zeros((1, s_dim), f32))
        _, h0 = _segment_inits(ends_c, _cpow(ar1, ai1, segc), zero1, reverse)
        bu_scr[...] = jnp.dot(u_bf, wb_ref[direction], preferred_element_type=f32)
        ends = _segment_scan(bu_scr, seg, s_dim, ar, ai, zero, reverse, store=False)
        inits, _ = _segment_inits(ends, _cpow(ar1, ai1, seg), h0, reverse)
        _segment_scan(bu_scr, seg, s_dim, ar, ai, inits, reverse, store=True)
        y_scr[...] += jnp.dot(bu_scr[...].astype(bf16), cm_ref[direction],
                              preferred_element_type=f32)

    for j in range(seg):
        yn_scr[pl.ds(j, ns, stride=seg), :] = y_scr[j * ns:(j + 1) * ns, :]
    z_ref[...] = jax.nn.gelu(yn_scr[...]).astype(z_ref.dtype)


def _s5_core(x_lat, x_ctx, m, layer, a_blk, wb_blk, cm_blk, d_skip, *, batch, n, nc):
    d = x_lat.shape[1]
    n_blk = d // _LANES
    s_dim = a_blk.shape[-1]
    assert n % (_SUBLANES * _SCAN_UNROLL) == 0 and nc % _SUBLANES == 0
    return pl.pallas_call(
        functools.partial(_s5_body, n=n, nc=nc, batch=batch),
        grid=(batch, n_blk),
        in_specs=[
            pl.BlockSpec((n, _LANES), lambda b, g: (b, g)),
            pl.BlockSpec((nc, _LANES), lambda b, g: (b, g)),
            pl.BlockSpec((None, _SUBLANES, _LANES), lambda b, g: (layer, 0, g)),
            pl.BlockSpec((None, _SUBLANES, _LANES), lambda b, g: (layer, 0, n_blk + g)),
            pl.BlockSpec((2, None, 2, s_dim), lambda b, g: (0, g, 0, 0)),
            pl.BlockSpec((2, None, _LANES, 2 * s_dim), lambda b, g: (0, g, 0, 0)),
            pl.BlockSpec((2, None, 2 * s_dim, _LANES), lambda b, g: (0, g, 0, 0)),
            pl.BlockSpec((1, _LANES), lambda b, g: (0, g)),
        ],
        out_specs=pl.BlockSpec((n, _LANES), lambda b, g: (b, g)),
        out_shape=jax.ShapeDtypeStruct((batch * n, d), bf16),
        scratch_shapes=[
            pltpu.VMEM((n, _LANES), f32),
            pltpu.VMEM((nc, _LANES), f32),
            pltpu.VMEM((n, 2 * s_dim), f32),
            pltpu.VMEM((nc, 2 * s_dim), f32),
            pltpu.VMEM((n, _LANES), f32),
            pltpu.VMEM((n, _LANES), f32),
        ],
        compiler_params=_cparams(("arbitrary", "arbitrary")),
        name="s5_core",
    )(x_lat, x_ctx, m, m, a_blk, wb_blk, cm_blk, d_skip.reshape(1, d))


def _s5_block_params(ar, ai, br, bi, c_re, c_im):
    _, n_groups, p_dim, gh = br.shape
    gpb = _LANES // gh
    n_blk = n_groups // gpb
    s_dim = gpb * p_dim
    eye = jnp.eye(gpb, dtype=f32)
    a_blk = jnp.stack([ar.reshape(2, n_blk, s_dim), ai.reshape(2, n_blk, s_dim)], axis=2)

    def b_mat(bx):
        t = bx.reshape(2, n_blk, gpb, p_dim, gh)
        return jnp.einsum("dngph,gk->dnghkp", t, eye).reshape(2, n_blk, gpb * gh, s_dim)

    def c_mat(cx):
        t = cx.reshape(2, n_blk, gpb, gh, p_dim)
        return jnp.einsum("dnghp,gk->dngpkh", t, eye).reshape(2, n_blk, s_dim, gpb * gh)

    wb = jnp.concatenate([b_mat(br), b_mat(bi)], axis=-1).astype(bf16)
    cm = jnp.concatenate([c_mat(c_re), -c_mat(c_im)], axis=-2).astype(bf16)
    return a_blk, wb, cm


def _moe_layer(h2_list, logits_list, dims, w_gate, w_up, w_down, layer, *, batch):
    idxs, gates, caps = [], [], []
    for lg, n in zip(logits_list, dims):
        idx, gate = _route(lg, batch=batch, n=n)
        idxs.append(idx)
        gates.append(gate)
        caps.append(idx.shape[-1])
    n_exp = w_gate.shape[1]
    idx_all = jnp.concatenate([jnp.swapaxes(i, 0, 1).reshape(n_exp, -1) for i in idxs], axis=1)
    gate_all = jnp.concatenate([jnp.swapaxes(g, 0, 1).reshape(n_exp, -1) for g in gates], axis=1)
    seg_counts = [batch * c for c in caps]
    idx_flat = idx_all.reshape(-1)
    ye = _moe_experts(idx_flat, h2_list, seg_counts, gate_all[..., None], w_gate, w_up, w_down,
                      layer)
    outs, slot_base = [], 0
    for n, cap in zip(dims, caps):
        outs.append(_combine(idx_flat, ye, batch=batch, n=n, cap=cap, slot_base=slot_base))
        slot_base += batch * cap
    return outs


def _forward(x, c, ctx, c_ctx, w_mod, b_mod, ln_g, ln_b, na_w_qkv, na_w_o, na_rpb, s5_lam_re,
             s5_lam_im, s5_log_step, s5_b_re, s5_b_im, s5_c_re, s5_c_im, s5_d, s5_w_val, s5_w_gate,
             moe_w_router, moe_w_gate, moe_w_up, moe_w_down, *, n_heads):
    batch, seq, d = x.shape
    ctx_len = ctx.shape[1]
    depth = w_mod.shape[0]
    assert depth == 2
    alpha = (2 * depth) ** 0.25

    cond = jnp.concatenate([c, c_ctx[None, :]], axis=0)
    m = _modulation(cond, w_mod, b_mod)
    x_lat = x.reshape(batch * seq, d)
    x_ctx = ctx.reshape(batch * ctx_len, d)
    lat = dict(rows_per_mod=seq, row0=0)
    cx = dict(rows_per_mod=batch * ctx_len, row0=batch)

    w_qkv = na_w_qkv[0].astype(bf16)
    w_o = na_w_o[0].astype(bf16)
    qkv_lat = _mod_proj(x_lat, m, 0, w_qkv, **lat)
    qkv_ctx = _mod_proj(x_ctx, m, 0, w_qkv, **cx)
    o_lat, o_ctx = _attention(qkv_lat, qkv_ctx, na_rpb[0], batch=batch, seq=seq, ctx_len=ctx_len,
                              n_heads=n_heads)
    wr_t = moe_w_router[0].T
    x1_lat, h2_lat, lg_lat = _mixer_post(o_lat, [w_o], x_lat, m, 0, ln_g[0, 0], ln_b[0, 0], wr_t,
                                         alpha=alpha, **lat)
    x1_ctx, h2_ctx, lg_ctx = _mixer_post(o_ctx, [w_o], x_ctx, m, 0, ln_g[0, 0], ln_b[0, 0], wr_t,
                                         alpha=alpha, **cx)
    f_lat, f_ctx = _moe_layer([h2_lat, h2_ctx], [lg_lat, lg_ctx], [seq, ctx_len], moe_w_gate,
                              moe_w_up, moe_w_down, 0, batch=batch)
    x_lat = _ffn_post(x1_lat, f_lat, m, 0, ln_g[0, 1], ln_b[0, 1], alpha=alpha, **lat)
    x_ctx = _ffn_post(x1_ctx, f_ctx, m, 0, ln_g[0, 1], ln_b[0, 1], alpha=alpha, **cx)

    ar, ai, br, bi = _s5_discretize(s5_lam_re[0], s5_lam_im[0], s5_log_step[0], s5_b_re[0],
                                    s5_b_im[0])
    a_blk, wb_blk, cm_blk = _s5_block_params(ar, ai, br, bi, s5_c_re[0], s5_c_im[0])
    z = _s5_core(x_lat, x_ctx, m, 1, a_blk, wb_blk, cm_blk, s5_d[0], batch=batch, n=seq,
                 nc=ctx_len)
    x1_lat, h2_lat, lg_lat = _mixer_post(z, [s5_w_val[0].astype(bf16), s5_w_gate[0].astype(bf16)],
                                         x_lat, m, 1, ln_g[1, 0], ln_b[1, 0], moe_w_router[1].T,
                                         alpha=alpha, **lat)
    (f_lat,) = _moe_layer([h2_lat], [lg_lat], [seq], moe_w_gate, moe_w_up, moe_w_down, 1,
                          batch=batch)
    x_lat = _ffn_post(x1_lat, f_lat, m, 1, ln_g[1, 1], ln_b[1, 1], alpha=alpha, **lat)
    return x_lat.reshape(batch, seq, d)


def kernel(x, c, ctx, c_ctx, w_mod, b_mod, ln_g, ln_b, na_w_qkv, na_w_o, na_rpb, s5_lam_re,
           s5_lam_im, s5_log_step, s5_b_re, s5_b_im, s5_c_re, s5_c_im, s5_d, s5_w_val, s5_w_gate,
           moe_w_router, moe_w_gate, moe_w_up, moe_w_down):
    return _forward(x, c, ctx, c_ctx, w_mod, b_mod, ln_g, ln_b, na_w_qkv, na_w_o, na_rpb,
                    s5_lam_re, s5_lam_im, s5_log_step, s5_b_re, s5_b_im, s5_c_re, s5_c_im, s5_d,
                    s5_w_val, s5_w_gate, moe_w_router, moe_w_gate, moe_w_up, moe_w_down,
                    n_heads=_N_HEADS)
```

```python
import functools
import math

import jax
import jax.numpy as jnp
from jax import lax
from jax.experimental import pallas as pl
from jax.experimental.pallas import tpu as pltpu

f32 = jnp.float32
bf16 = jnp.bfloat16
i32 = jnp.int32

_GRID_W = 64
_WIN_R = 8
_WIN_C = 16
_N_HEADS = 16
_SSM_GROUP = 16
_CAPACITY_FACTOR = 2
_LN_EPS = 1e-5
_NEG_INF = -1e30

_LANES = 128
_SUBLANES = 8
_VMEM_LIMIT_BYTES = 56 * 1024 * 1024

_NT_DIMS = (((1,), (1,)), ((), ()))


def _cparams(sem):
    return pltpu.CompilerParams(dimension_semantics=sem, vmem_limit_bytes=_VMEM_LIMIT_BYTES)


_MOD_UNROLL = 4


def _mod_body(cb_ref, w_ref, b_ref, o_ref, s_scr, *, n_rows):
    d, tn = w_ref.shape
    reps = tn // _LANES

    @pl.when((pl.program_id(0) == 0) & (pl.program_id(1) == 0))
    def _():
        cv = cb_ref[...]
        s_scr[...] = cv * jax.nn.sigmoid(cv)

    def step(i, accs):
        accs = list(accs)
        for v in range(_MOD_UNROLL):
            k0 = pl.multiple_of((i * _MOD_UNROLL + v) * _SUBLANES, _SUBLANES)
            w = w_ref[pl.ds(k0, _SUBLANES), :]
            for r in range(n_rows):
                s = s_scr[r, pl.ds(k0, _SUBLANES), :]
                accs[r] = accs[r] + jnp.tile(s, (1, reps)) * w
        return tuple(accs)

    init = tuple(jnp.zeros((_SUBLANES, tn), f32) for _ in range(n_rows))
    accs = lax.fori_loop(0, d // (_SUBLANES * _MOD_UNROLL), step, init)
    rows = [jnp.sum(a, axis=0, keepdims=True) + b_ref[...] for a in accs]
    rows.append(jnp.zeros((_SUBLANES - n_rows, tn), f32))
    o_ref[...] = jnp.concatenate(rows, axis=0)


def _modulation(cond, w_mod, b_mod):
    n_rows, d = cond.shape
    depth, _, n6 = w_mod.shape
    tn = min(1024, n6)
    cb = jnp.broadcast_to(cond[:, :, None], (n_rows, d, _LANES))
    return pl.pallas_call(
        functools.partial(_mod_body, n_rows=n_rows),
        grid=(depth, n6 // tn),
        in_specs=[
            pl.BlockSpec((n_rows, d, _LANES), lambda l, j: (0, 0, 0)),
            pl.BlockSpec((None, d, tn), lambda l, j: (l, 0, j)),
            pl.BlockSpec((None, 1, tn), lambda l, j: (l, 0, j)),
        ],
        out_specs=pl.BlockSpec((None, _SUBLANES, tn), lambda l, j: (l, 0, j)),
        out_shape=jax.ShapeDtypeStruct((depth, _SUBLANES, n6), f32),
        scratch_shapes=[pltpu.VMEM((n_rows, d, _LANES), f32)],
        compiler_params=_cparams(("arbitrary", "arbitrary")),
        name="modulation",
    )(cb, w_mod, b_mod.reshape(depth, 1, n6))


def _mod_spec(layer, chunk, d, nargs):
    if nargs == 1:
        return pl.BlockSpec((None, _SUBLANES, d), lambda i: (layer, 0, chunk))
    return pl.BlockSpec((None, _SUBLANES, d), lambda i, j: (layer, 0, chunk))


def _proj_body(x_ref, sc_ref, sh_ref, w_ref, o_ref, h_scr, *, tiles_per_row, row0):
    @pl.when(pl.program_id(1) == 0)
    def _():
        r = row0 + pl.program_id(0) // tiles_per_row
        sc = sc_ref[pl.ds(r, 1), :]
        sh = sh_ref[pl.ds(r, 1), :]
        h_scr[...] = (x_ref[...] * (1.0 + sc) + sh).astype(bf16)

    o_ref[...] = jnp.dot(h_scr[...], w_ref[...], preferred_element_type=f32).astype(o_ref.dtype)


def _mod_proj(x, m, layer, w_bf, *, rows_per_mod, row0):
    mtot, d = x.shape
    n = w_bf.shape[1]
    tm = min(1024, rows_per_mod, mtot)
    tn = min(512, n)
    return pl.pallas_call(
        functools.partial(_proj_body, tiles_per_row=rows_per_mod // tm, row0=row0),
        grid=(mtot // tm, n // tn),
        in_specs=[
            pl.BlockSpec((tm, d), lambda i, j: (i, 0)),
            _mod_spec(layer, 1, d, 2),
            _mod_spec(layer, 0, d, 2),
            pl.BlockSpec((d, tn), lambda i, j: (0, j)),
        ],
        out_specs=pl.BlockSpec((tm, tn), lambda i, j: (i, j)),
        out_shape=jax.ShapeDtypeStruct((mtot, n), bf16),
        scratch_shapes=[pltpu.VMEM((tm, d), bf16)],
        compiler_params=_cparams(("arbitrary", "arbitrary")),
        name="mod_proj",
    )(x, m, m, w_bf)


_ATTN_PAIR_UNROLL = 4
_ATTN_VT_UNROLL = 4
_ATTN_KEY_ROWS = _WIN_R + 2


def _attn_body(rpb_ref, q_ref, k_ref, v_ref, qc_ref, kc_ref, vc_ref, o_ref, oc_ref, t_scr, vt_scr,
               *, rows, dh):
    w = _GRID_W
    n_dr = 2 * _WIN_R - 1
    n_dc = 2 * _WIN_C - 1
    masked = n_dr
    h = pl.program_id(0)
    scale = dh ** -0.5
    lane = lax.broadcasted_iota(i32, (w, 2 * w), 1)
    first_row = lane < w

    @pl.when(pl.program_id(1) == 0)
    def _build():
        kc = lax.broadcasted_iota(i32, (w, 2 * w), 0)
        c = lane & (w - 1)
        delta = jnp.clip(kc - c + (_WIN_C - 1), 0, n_dc - 1)
        cstart = jnp.clip(c - _WIN_C // 2, 0, w - _WIN_C)
        valid = (kc >= cstart) & (kc < cstart + _WIN_C)
        base = h * (n_dr * n_dc)

        def dr_body(dr, carry):
            acc = jnp.zeros((w, 2 * w), f32)
            for d in range(n_dc):
                acc = jnp.where(delta == d, rpb_ref[base + dr * n_dc + d], acc)
            t_scr[dr] = jnp.where(valid, acc, _NEG_INF)
            return carry

        lax.fori_loop(0, n_dr, dr_body, 0)
        t_scr[masked] = jnp.full((w, 2 * w), _NEG_INF, f32)

    def vt_body(i, carry):
        for u in range(_ATTN_VT_UNROLL):
            blk_i = i * _ATTN_VT_UNROLL + u
            blk = v_ref[pl.ds(pl.multiple_of(blk_i * 2 * w, 2 * w), 2 * w), :]
            vt_scr[blk_i] = blk.astype(f32).T.astype(bf16)
        return carry

    lax.fori_loop(0, rows // (2 * _ATTN_VT_UNROLL), vt_body, 0)
    kcx = kc_ref[...]
    vcx = vc_ref[...]
    vct = vcx.astype(f32).T.astype(bf16)

    def pair_scores(i):
        r = 2 * i
        band = jnp.minimum(jnp.clip(r - _WIN_R // 2, 0, rows - _WIN_R - 1) & -2,
                           rows - _ATTN_KEY_ROWS)
        q0 = pl.multiple_of(r * w, 2 * w)
        k0 = pl.multiple_of(band * w, 2 * w)
        q2 = q_ref[pl.ds(q0, 2 * w), :]
        kw = k_ref[pl.ds(k0, _ATTN_KEY_ROWS * w), :]
        blocks = []
        drs = []
        for u in range(2):
            rs = jnp.clip(r + u - _WIN_R // 2, 0, rows - _WIN_R)
            drs.append((rs - (r + u) + (_WIN_R - 1), rs - band))
        for kr in range(_ATTN_KEY_ROWS):
            idx = []
            for s0, off in drs:
                rel = kr - off
                idx.append(jnp.where((rel >= 0) & (rel < _WIN_R), s0 + rel, masked))
            blocks.append(jnp.where(first_row, t_scr[idx[0]], t_scr[idx[1]]))
        bias = jnp.concatenate(blocks, axis=0)
        s_loc = lax.dot_general(kw, q2, _NT_DIMS, preferred_element_type=f32) * scale + bias
        s_ctx = lax.dot_general(kcx, q2, _NT_DIMS, preferred_element_type=f32) * scale
        return s_loc, s_ctx, band, q0

    def pair_output(s_loc, s_ctx, band, q0):
        mx = jnp.maximum(jnp.max(s_loc, axis=0, keepdims=True),
                         jnp.max(s_ctx, axis=0, keepdims=True))
        p_loc = jnp.exp(s_loc - mx)
        p_ctx = jnp.exp(s_ctx - mx)
        den = jnp.sum(p_loc, axis=0, keepdims=True) + jnp.sum(p_ctx, axis=0, keepdims=True)
        vb = band // 2
        vwt = jnp.concatenate([vt_scr[vb + t] for t in range(_ATTN_KEY_ROWS // 2)], axis=1)
        o_t = (jnp.dot(vwt, p_loc.astype(bf16), preferred_element_type=f32)
               + jnp.dot(vct, p_ctx.astype(bf16), preferred_element_type=f32))
        o_ref[pl.ds(q0, 2 * w), :] = (o_t / den).T.astype(o_ref.dtype)

    def pair_body(i, carry):
        scores = [pair_scores(i * _ATTN_PAIR_UNROLL + u) for u in range(_ATTN_PAIR_UNROLL)]
        for sc in scores:
            pair_output(*sc)
        return carry

    lax.fori_loop(0, rows // (2 * _ATTN_PAIR_UNROLL), pair_body, 0)

    s = lax.dot_general(qc_ref[...], kcx, _NT_DIMS, preferred_element_type=f32) * scale
    p = jnp.exp(s - jnp.max(s, axis=1, keepdims=True))
    den = jnp.sum(p, axis=1, keepdims=True)
    oc = jnp.dot(p.astype(bf16), vcx, preferred_element_type=f32)
    oc_ref[...] = (oc / den).astype(oc_ref.dtype)


def _attention(qkv_lat, qkv_ctx, rpb, *, batch, seq, ctx_len, n_heads):
    d = qkv_lat.shape[1] // 3
    dh = d // n_heads
    rows = seq // _GRID_W
    assert dh == _LANES and _GRID_W * 2 == _LANES and rows >= _WIN_R
    assert rows % (2 * _ATTN_PAIR_UNROLL) == 0 and rows >= _ATTN_KEY_ROWS
    assert rows % (2 * _ATTN_VT_UNROLL) == 0
    n_dr = 2 * _WIN_R - 1
    return pl.pallas_call(
        functools.partial(_attn_body, rows=rows, dh=dh),
        grid=(n_heads, batch),
        in_specs=[
            pl.BlockSpec(memory_space=pltpu.SMEM),
            pl.BlockSpec((seq, dh), lambda h, b: (b, h)),
            pl.BlockSpec((seq, dh), lambda h, b: (b, n_heads + h)),
            pl.BlockSpec((seq, dh), lambda h, b: (b, 2 * n_heads + h)),
            pl.BlockSpec((ctx_len, dh), lambda h, b: (b, h)),
            pl.BlockSpec((ctx_len, dh), lambda h, b: (b, n_heads + h)),
            pl.BlockSpec((ctx_len, dh), lambda h, b: (b, 2 * n_heads + h)),
        ],
        out_specs=[
            pl.BlockSpec((seq, dh), lambda h, b: (b, h)),
            pl.BlockSpec((ctx_len, dh), lambda h, b: (b, h)),
        ],
        out_shape=[
            jax.ShapeDtypeStruct((batch * seq, d), bf16),
            jax.ShapeDtypeStruct((batch * ctx_len, d), bf16),
        ],
        scratch_shapes=[
            pltpu.VMEM((n_dr + 1, _GRID_W, 2 * _GRID_W), f32),
            pltpu.VMEM((rows // 2, dh, 2 * _GRID_W), bf16),
        ],
        compiler_params=_cparams(("arbitrary", "arbitrary")),
        name="nbr_attention",
    )(rpb.reshape(-1), qkv_lat, qkv_lat, qkv_lat, qkv_ctx, qkv_ctx, qkv_ctx)


def _layer_norm(y, g, b):
    mu = jnp.mean(y, axis=-1, keepdims=True)
    yc = y - mu
    var = jnp.mean(yc * yc, axis=-1, keepdims=True)
    return yc * lax.rsqrt(var + _LN_EPS) * g + b


def _post_body(*refs, glu, alpha, tiles_per_row, row0):
    if glu:
        a_ref, wv_ref, wg_ref = refs[:3]
        rest = refs[3:]
    else:
        a_ref, wv_ref = refs[:2]
        rest = refs[2:]
    x_ref, g1_ref, sc2_ref, sh2_ref, lng_ref, lnb_ref, wr_ref, x1_ref, h2_ref, lg_ref = rest
    a = a_ref[...]
    o = jnp.dot(a, wv_ref[...], preferred_element_type=f32)
    if glu:
        o = o * jax.nn.sigmoid(jnp.dot(a, wg_ref[...], preferred_element_type=f32))
    r = row0 + pl.program_id(0) // tiles_per_row
    g1 = g1_ref[pl.ds(r, 1), :]
    x1 = _layer_norm(alpha * x_ref[...] + g1 * o, lng_ref[...], lnb_ref[...])
    h2 = x1 * (1.0 + sc2_ref[pl.ds(r, 1), :]) + sh2_ref[pl.ds(r, 1), :]
    x1_ref[...] = x1
    h2_ref[...] = _pack_bf16_pairs(h2)
    n_exp = wr_ref.shape[0]
    wr = wr_ref[...]
    w_hi = wr.astype(bf16)
    w_lo = (wr - w_hi.astype(f32)).astype(bf16)
    h_hi = h2.astype(bf16)
    h_lo = (h2 - h_hi.astype(f32)).astype(bf16)
    both = lax.dot_general(jnp.concatenate([w_hi, w_lo], axis=0), h_hi, _NT_DIMS,
                           preferred_element_type=f32)
    lg_ref[...] = (both[:n_exp] + both[n_exp:]
                   + lax.dot_general(w_hi, h_lo, _NT_DIMS, preferred_element_type=f32))


def _mixer_post(a_bf, weights_bf, x, m, layer, ln_g, ln_b, w_router_t, *, alpha, rows_per_mod,
                row0):
    mtot, d = x.shape
    n_exp = w_router_t.shape[0]
    glu = len(weights_bf) == 2
    tm = min(256 if glu else 512, mtot, rows_per_mod)
    const = lambda i: (0, 0)
    in_specs = [pl.BlockSpec((tm, d), lambda i: (i, 0))]
    in_specs += [pl.BlockSpec((d, d), const, pipeline_mode=pl.Buffered(1)) for _ in weights_bf]
    in_specs += [
        pl.BlockSpec((tm, d), lambda i: (i, 0)),
        _mod_spec(layer, 2, d, 1),
        _mod_spec(layer, 4, d, 1),
        _mod_spec(layer, 3, d, 1),
        pl.BlockSpec((1, d), const),
        pl.BlockSpec((1, d), const),
        pl.BlockSpec((n_exp, d), const),
    ]
    return pl.pallas_call(
        functools.partial(_post_body, glu=glu, alpha=alpha, tiles_per_row=rows_per_mod // tm,
                          row0=row0),
        grid=(mtot // tm,),
        in_specs=in_specs,
        out_specs=[
            pl.BlockSpec((tm, d), lambda i: (i, 0)),
            pl.BlockSpec((tm, d // 2), lambda i: (i, 0)),
            pl.BlockSpec((n_exp, tm), lambda i: (0, i)),
        ],
        out_shape=[
            jax.ShapeDtypeStruct((mtot, d), f32),
            jax.ShapeDtypeStruct((mtot, d // 2), jnp.uint32),
            jax.ShapeDtypeStruct((n_exp, mtot), f32),
        ],
        compiler_params=_cparams(("arbitrary",)),
        name="mixer_post_glu" if glu else "mixer_post",
    )(a_bf, *weights_bf, x, m, m, m, ln_g.reshape(1, d), ln_b.reshape(1, d), w_router_t)


def _cumsum_excl(mask01):
    rows, n = mask01.shape
    r = lax.broadcasted_iota(i32, (_LANES, _LANES), 0)
    c = lax.broadcasted_iota(i32, (_LANES, _LANES), 1)
    tri = jnp.where(r < c, 1.0, 0.0).astype(bf16)
    carry = jnp.zeros((rows, 1), f32)
    outs = []
    for t in range(n // _LANES):
        blk = mask01[:, t * _LANES:(t + 1) * _LANES]
        outs.append(jnp.dot(blk.astype(bf16), tri, preferred_element_type=f32) + carry)
        carry = carry + jnp.sum(blk, axis=1, keepdims=True)
    return outs[0] if len(outs) == 1 else jnp.concatenate(outs, axis=1)


def _route_body(lg_ref, idx_ref, gate_ref, posm_scr, aff_scr, *, n, cap, chunk):
    n_exp = lg_ref.shape[0]
    x = lg_ref[...]
    ex = jnp.exp(x - jnp.max(x, axis=0, keepdims=True))
    aff = ex / jnp.sum(ex, axis=0, keepdims=True)
    bits = pltpu.bitcast(aff, i32)

    thr = jnp.zeros((n_exp, 1), i32)
    for bit in range(30, -1, -1):
        cand = thr | (1 << bit)
        cnt = jnp.sum(jnp.where(bits >= cand, 1.0, 0.0), axis=1, keepdims=True)
        thr = jnp.where(cnt >= cap, cand, thr)
    gt = bits > thr
    eq = bits == thr
    need = cap - jnp.sum(jnp.where(gt, 1.0, 0.0), axis=1, keepdims=True)
    rank_eq = _cumsum_excl(jnp.where(eq, 1.0, 0.0))
    sel = gt | (eq & (rank_eq < need))
    pos = _cumsum_excl(jnp.where(sel, 1.0, 0.0))
    posm_scr[...] = jnp.where(sel, pos, -1.0)
    aff_scr[...] = aff
    tok = lax.broadcasted_iota(i32, (chunk, n), 1).astype(f32)
    tok_base = pl.program_id(0) * n

    def per_expert(e, carry):
        prow = posm_scr[pl.ds(e, 1), :]
        arow = aff_scr[pl.ds(e, 1), :]
        for pc in range(cap // chunk):
            slot = (lax.broadcasted_iota(i32, (chunk, n), 0) + pc * chunk).astype(f32)
            hit = prow == slot
            idx = jnp.sum(jnp.where(hit, tok, 0.0), axis=1, keepdims=True)
            gate = jnp.sum(jnp.where(hit, arow, 0.0), axis=1, keepdims=True)
            idx_ref[e, pl.ds(pc * chunk, chunk), :] = idx.astype(i32) + tok_base
            gate_ref[e, pl.ds(pc * chunk, chunk), :] = gate
        return carry

    lax.fori_loop(0, n_exp, per_expert, 0)


def _route(logits_t, *, batch, n):
    n_exp = logits_t.shape[0]
    cap = _CAPACITY_FACTOR * n // n_exp
    chunk = min(_LANES, cap)
    assert n % _LANES == 0 and cap % chunk == 0
    idx, gate = pl.pallas_call(
        functools.partial(_route_body, n=n, cap=cap, chunk=chunk),
        grid=(batch,),
        in_specs=[pl.BlockSpec((n_exp, n), lambda b: (0, b))],
        out_specs=[
            pl.BlockSpec((None, n_exp, cap, 1), lambda b: (b, 0, 0, 0)),
            pl.BlockSpec((None, n_exp, cap, 1), lambda b: (b, 0, 0, 0)),
        ],
        out_shape=[
            jax.ShapeDtypeStruct((batch, n_exp, cap, 1), i32),
            jax.ShapeDtypeStruct((batch, n_exp, cap, 1), f32),
        ],
        scratch_shapes=[pltpu.VMEM((n_exp, n), f32), pltpu.VMEM((n_exp, n), f32)],
        compiler_params=_cparams(("arbitrary",)),
        name="route_topk",
    )(logits_t)
    return idx[..., 0], gate[..., 0]


def _pack_bf16_pairs(h):
    half = h.shape[1] // 2
    u = pltpu.bitcast(h.astype(bf16).astype(f32), jnp.uint32)
    return (u[:, :half] & jnp.uint32(0xFFFF0000)) | (u[:, half:] >> 16)


def _unpack_bf16_pairs(p):
    hi = pltpu.bitcast(p & jnp.uint32(0xFFFF0000), f32).astype(bf16)
    lo = pltpu.bitcast(p << 16, f32).astype(bf16)
    return jnp.concatenate([hi, lo], axis=1)


def _moe_body(idx_ref, *refs, seg_counts, m_slots, n_tiles):
    n_src = len(seg_counts)
    srcs = refs[:n_src]
    gate_ref, wg_ref, wu_ref, wd_ref, o_ref, xp_scr, xe16, hmid, sem = refs[n_src:]
    e = pl.program_id(0)
    j = pl.program_id(1)
    tf = wg_ref.shape[1]

    def row_copy(src, row, blk, sub):
        return pltpu.make_async_copy(src.at[pl.ds(row, 1), :], xp_scr.at[blk, pl.ds(sub, 1), :],
                                     sem)

    def for_each_slot(fn):
        slot0 = 0
        for src, cnt in zip(srcs, seg_counts):
            def body(i, c, src=src, slot0=slot0):
                for u in range(_SUBLANES):
                    fn(src, slot0 // _SUBLANES + i, u)
                return c

            lax.fori_loop(0, cnt // _SUBLANES, body, 0)
            slot0 += cnt

    def issue_gather(expert):
        base = expert * m_slots
        for_each_slot(lambda src, blk, sub: row_copy(
            src, idx_ref[base + blk * _SUBLANES + sub], blk, sub).start())

    @pl.when(j == 0)
    def _rows():
        @pl.when(e == 0)
        def _():
            issue_gather(0)

        for_each_slot(lambda src, blk, sub: row_copy(src, 0, blk, sub).wait())
        xe16[...] = _unpack_bf16_pairs(xp_scr[...].reshape(m_slots, xp_scr.shape[-1]))

        @pl.when(e + 1 < pl.num_programs(0))
        def _():
            issue_gather(e + 1)

    @pl.when(j < n_tiles)
    def _up():
        x = xe16[...]
        a = jnp.dot(x, wg_ref[...].astype(bf16), preferred_element_type=f32)
        u = jnp.dot(x, wu_ref[...].astype(bf16), preferred_element_type=f32)
        hmid[j] = (a * jax.nn.sigmoid(a) * u).astype(bf16)

    @pl.when(j >= n_tiles)
    def _down():
        y = jnp.dot(hmid[0], wd_ref[0:tf, :].astype(bf16), preferred_element_type=f32)
        for k in range(1, n_tiles):
            y += jnp.dot(hmid[k], wd_ref[k * tf:(k + 1) * tf, :].astype(bf16),
                         preferred_element_type=f32)
        o_ref[...] = y * gate_ref[...]


def _moe_experts(idx_flat, srcs, seg_counts, gates, w_gate, w_up, w_down, layer):
    _, n_exp, d, ff = w_gate.shape
    m_slots = sum(seg_counts)
    tf = min(512, ff)
    n_tiles = ff // tf
    assert d // tf == n_tiles and all(c % _SUBLANES == 0 for c in seg_counts)
    n_src = len(srcs)
    up_idx = lambda e, j, idx: (layer, e, 0, jnp.minimum(j, n_tiles - 1))
    dn_idx = lambda e, j, idx: (layer, e, 0, jnp.maximum(j - n_tiles, 0))
    grid_spec = pltpu.PrefetchScalarGridSpec(
        num_scalar_prefetch=1,
        grid=(n_exp, 2 * n_tiles),
        in_specs=[pl.BlockSpec(memory_space=pl.ANY)] * n_src + [
            pl.BlockSpec((None, m_slots, 1), lambda e, j, idx: (e, 0, 0)),
            pl.BlockSpec((None, None, d, tf), up_idx),
            pl.BlockSpec((None, None, d, tf), up_idx),
            pl.BlockSpec((None, None, ff, tf), dn_idx),
        ],
        out_specs=pl.BlockSpec((None, m_slots, tf),
                               lambda e, j, idx: (e, 0, jnp.maximum(j - n_tiles, 0))),
        scratch_shapes=[
            pltpu.VMEM((m_slots // _SUBLANES, _SUBLANES, d // 2), jnp.uint32),
            pltpu.VMEM((m_slots, d), bf16),
            pltpu.VMEM((n_tiles, m_slots, tf), bf16),
            pltpu.SemaphoreType.DMA(()),
        ],
    )
    return pl.pallas_call(
        functools.partial(_moe_body, seg_counts=tuple(seg_counts), m_slots=m_slots,
                          n_tiles=n_tiles),
        grid_spec=grid_spec,
        out_shape=jax.ShapeDtypeStruct((n_exp, m_slots, d), f32),
        compiler_params=_cparams(("arbitrary", "arbitrary")),
        name="moe_experts",
    )(idx_flat, *srcs, gates, w_gate, w_up, w_down)


_SCATTER_UNROLL = 2


def _combine_body(idx_ref, ye_ref, o_hbm, acc, sem, *, m_slots, slot_base, cap, n):
    b = pl.program_id(0)
    e = pl.program_id(1)

    @pl.when(e == 0)
    def _():
        acc[...] = jnp.zeros_like(acc)

    base = e * m_slots + slot_base + b * cap
    tok0 = b * n

    def body(i, carry):
        j0 = i * _SCATTER_UNROLL
        toks = [idx_ref[base + j0 + u] - tok0 for u in range(_SCATTER_UNROLL)]
        vals = [acc[pl.ds(toks[u], 1), :] + ye_ref[pl.ds(j0 + u, 1), :]
                for u in range(_SCATTER_UNROLL)]
        for u in range(_SCATTER_UNROLL):
            acc[pl.ds(toks[u], 1), :] = vals[u]
        return carry

    lax.fori_loop(0, cap // _SCATTER_UNROLL, body, 0)

    @pl.when(e == pl.num_programs(1) - 1)
    def _():
        out = pltpu.make_async_copy(acc, o_hbm.at[pl.ds(pl.multiple_of(b * n, n), n), :], sem)
        out.start()
        out.wait()


def _combine(idx_flat, ye, *, batch, n, cap, slot_base):
    n_exp, m_slots, d = ye.shape
    assert slot_base % cap == 0 and cap % _SCATTER_UNROLL == 0
    blk0 = slot_base // cap
    grid_spec = pltpu.PrefetchScalarGridSpec(
        num_scalar_prefetch=1,
        grid=(batch, n_exp),
        in_specs=[pl.BlockSpec((None, cap, d), lambda b, e, idx: (e, blk0 + b, 0))],
        out_specs=pl.BlockSpec(memory_space=pl.ANY),
        scratch_shapes=[pltpu.VMEM((n, d), f32), pltpu.SemaphoreType.DMA(())],
    )
    return pl.pallas_call(
        functools.partial(_combine_body, m_slots=m_slots, slot_base=slot_base, cap=cap, n=n),
        grid_spec=grid_spec,
        out_shape=jax.ShapeDtypeStruct((batch * n, d), f32),
        compiler_params=_cparams(("arbitrary", "arbitrary")),
        name="moe_combine",
    )(idx_flat, ye)


def _ffn_post_body(x_ref, f_ref, g2_ref, lng_ref, lnb_ref, o_ref, *, alpha, tiles_per_row, row0):
    r = row0 + pl.program_id(0) // tiles_per_row
    g2 = g2_ref[pl.ds(r, 1), :]
    o_ref[...] = _layer_norm(alpha * x_ref[...] + g2 * f_ref[...], lng_ref[...], lnb_ref[...])


def _ffn_post(x1, fo, m, layer, ln_g, ln_b, *, alpha, rows_per_mod, row0):
    mtot, d = x1.shape
    tm = min(512, mtot, rows_per_mod)
    const = lambda i: (0, 0)
    return pl.pallas_call(
        functools.partial(_ffn_post_body, alpha=alpha, tiles_per_row=rows_per_mod // tm, row0=row0),
        grid=(mtot // tm,),
        in_specs=[
            pl.BlockSpec((tm, d), lambda i: (i, 0)),
            pl.BlockSpec((tm, d), lambda i: (i, 0)),
            _mod_spec(layer, 5, d, 1),
            pl.BlockSpec((1, d), const),
            pl.BlockSpec((1, d), const),
        ],
        out_specs=pl.BlockSpec((tm, d), lambda i: (i, 0)),
        out_shape=jax.ShapeDtypeStruct((mtot, d), f32),
        compiler_params=_cparams(("arbitrary",)),
        name="ffn_post",
    )(x1, fo, m, ln_g.reshape(1, d), ln_b.reshape(1, d))


def _disc_body(lr_ref, li_ref, ls_ref, bre_ref, bim_ref, ar_ref, ai_ref, br_ref, bi_ref):
    lr = jnp.minimum(lr_ref[...], -1e-4)
    li = li_ref[...]
    dt = jnp.exp(ls_ref[...])
    mag = jnp.exp(lr * dt)
    ar = mag * jnp.cos(li * dt)
    ai = mag * jnp.sin(li * dt)
    nr = ar - 1.0
    den = lr * lr + li * li
    cr = (nr * lr + ai * li) / den
    ci = (ai * lr - nr * li) / den
    ar_ref[...] = ar
    ai_ref[...] = ai
    br_ref[...] = cr * bre_ref[...] - ci * bim_ref[...]
    bi_ref[...] = cr * bim_ref[...] + ci * bre_ref[...]


def _s5_discretize(lam_re, lam_im, log_step, b_re, b_im):
    shape = b_re.shape
    flat = (shape[0] * shape[1], shape[2] * shape[3])
    bc = lambda a: jnp.broadcast_to(a, shape).reshape(flat)
    outs = pl.pallas_call(
        _disc_body,
        out_shape=[jax.ShapeDtypeStruct(flat, f32)] * 4,
        name="s5_discretize",
    )(bc(lam_re[..., None]), bc(lam_im[..., None]), bc(log_step[:, :, None, None]),
      b_re.reshape(flat), b_im.reshape(flat))
    ar, ai, br, bi = [o.reshape(shape) for o in outs]
    return ar[..., 0], ai[..., 0], br, bi


_SCAN_UNROLL = 8


def _cmul(ar, ai, xr, xi):
    return ar * xr - ai * xi, ar * xi + ai * xr


def _cpow(ar, ai, k):
    rr, ri = jnp.ones_like(ar), jnp.zeros_like(ai)
    while k:
        if k & 1:
            rr, ri = _cmul(ar, ai, rr, ri)
        ar, ai = _cmul(ar, ai, ar, ai)
        k >>= 1
    return rr, ri


def _segment_scan(bu_ref, seg_len, s_dim, ar, ai, init, reverse, store):
    ns = _SUBLANES
    unroll = math.gcd(seg_len, _SCAN_UNROLL)

    def outer(jo, carry):
        xr, xi = carry
        for ji in range(unroll):
            jj = jo * unroll + ji
            j = (seg_len - 1 - jj) if reverse else jj
            r0 = pl.multiple_of(j * ns, ns)
            br = bu_ref[pl.ds(r0, ns), 0:s_dim]
            bi = bu_ref[pl.ds(r0, ns), s_dim:2 * s_dim]
            xr, xi = ar * xr - ai * xi + br, ar * xi + ai * xr + bi
            if store:
                bu_ref[pl.ds(r0, ns), 0:s_dim] = xr
                bu_ref[pl.ds(r0, ns), s_dim:2 * s_dim] = xi
        return xr, xi

    return lax.fori_loop(0, seg_len // unroll, outer, init)


def _segment_inits(ends, a_len, h0, reverse):
    er, ei = ends
    alr, ali = a_len
    ns = _SUBLANES
    order = range(ns - 1, -1, -1) if reverse else range(ns)
    cr, ci = h0
    inits_r, inits_i = [None] * ns, [None] * ns
    for s in order:
        inits_r[s], inits_i[s] = cr, ci
        pr, pi = _cmul(alr, ali, cr, ci)
        cr, ci = pr + er[s:s + 1, :], pi + ei[s:s + 1, :]
    return (jnp.concatenate(inits_r, axis=0), jnp.concatenate(inits_i, axis=0)), (cr, ci)


def _s5_body(x_ref, xc_ref, sh_ref, sc_ref, a_ref, wb_ref, cm_ref, d_ref, z_ref,
             up_scr, ucp_scr, bu_scr, buc_scr, y_scr, yn_scr, *, n, nc, batch):
    ns = _SUBLANES
    seg, segc = n // ns, nc // ns
    s_dim = a_ref.shape[-1]
    b = pl.program_id(0)
    sc_l, sh_l = sc_ref[pl.ds(b, 1), :], sh_ref[pl.ds(b, 1), :]
    sc_c, sh_c = sc_ref[batch:batch + 1, :], sh_ref[batch:batch + 1, :]

    for j in range(seg):
        up_scr[j * ns:(j + 1) * ns, :] = x_ref[pl.ds(j, ns, stride=seg), :] * (1.0 + sc_l) + sh_l
    for j in range(segc):
        ucp_scr[j * ns:(j + 1) * ns, :] = xc_ref[pl.ds(j, ns, stride=segc), :] * (1.0 + sc_c) + sh_c

    u = up_scr[...]
    y_scr[...] = d_ref[...] * u
    u_bf = u.astype(bf16)
    uc_bf = ucp_scr[...].astype(bf16)
    zero = (jnp.zeros((ns, s_dim), f32), jnp.zeros((ns, s_dim), f32))
    for direction in range(2):
        reverse = direction == 1
        ar1, ai1 = a_ref[direction, 0:1, :], a_ref[direction, 1:2, :]
        ar = jnp.broadcast_to(ar1, (ns, s_dim))
        ai = jnp.broadcast_to(ai1, (ns, s_dim))
        buc_scr[...] = jnp.dot(uc_bf, wb_ref[direction], preferred_element_type=f32)
        ends_c = _segment_scan(buc_scr, segc, s_dim, ar, ai, zero, reverse, store=False)
        zero1 = (jnp.zeros((1, s_dim), f32), jnp.zeros((1, s_dim), f32))
        _, h0 = _segment_inits(ends_c, _cpow(ar1, ai1, segc), zero1, reverse)
        bu_scr[...] = jnp.dot(u_bf, wb_ref[direction], preferred_element_type=f32)
        ends = _segment_scan(bu_scr, seg, s_dim, ar, ai, zero, reverse, store=False)
        inits, _ = _segment_inits(ends, _cpow(ar1, ai1, seg), h0, reverse)
        _segment_scan(bu_scr, seg, s_dim, ar, ai, inits, reverse, store=True)
        y_scr[...] += jnp.dot(bu_scr[...].astype(bf16), cm_ref[direction],
                              preferred_element_type=f32)

    for j in range(seg):
        yn_scr[pl.ds(j, ns, stride=seg), :] = y_scr[j * ns:(j + 1) * ns, :]
    z_ref[...] = jax.nn.gelu(yn_scr[...]).astype(z_ref.dtype)


def _s5_core(x_lat, x_ctx, m, layer, a_blk, wb_blk, cm_blk, d_skip, *, batch, n, nc):
    d = x_lat.shape[1]
    n_blk = d // _LANES
    s_dim = a_blk.shape[-1]
    assert n % (_SUBLANES * _SCAN_UNROLL) == 0 and nc % _SUBLANES == 0
    return pl.pallas_call(
        functools.partial(_s5_body, n=n, nc=nc, batch=batch),
        grid=(batch, n_blk),
        in_specs=[
            pl.BlockSpec((n, _LANES), lambda b, g: (b, g)),
            pl.BlockSpec((nc, _LANES), lambda b, g: (b, g)),
            pl.BlockSpec((None, _SUBLANES, _LANES), lambda b, g: (layer, 0, g)),
            pl.BlockSpec((None, _SUBLANES, _LANES), lambda b, g: (layer, 0, n_blk + g)),
            pl.BlockSpec((2, None, 2, s_dim), lambda b, g: (0, g, 0, 0)),
            pl.BlockSpec((2, None, _LANES, 2 * s_dim), lambda b, g: (0, g, 0, 0)),
            pl.BlockSpec((2, None, 2 * s_dim, _LANES), lambda b, g: (0, g, 0, 0)),
            pl.BlockSpec((1, _LANES), lambda b, g: (0, g)),
        ],
        out_specs=pl.BlockSpec((n, _LANES), lambda b, g: (b, g)),
        out_shape=jax.ShapeDtypeStruct((batch * n, d), bf16),
        scratch_shapes=[
            pltpu.VMEM((n, _LANES), f32),
            pltpu.VMEM((nc, _LANES), f32),
            pltpu.VMEM((n, 2 * s_dim), f32),
            pltpu.VMEM((nc, 2 * s_dim), f32),
            pltpu.VMEM((n, _LANES), f32),
            pltpu.VMEM((n, _LANES), f32),
        ],
        compiler_params=_cparams(("arbitrary", "arbitrary")),
        name="s5_core",
    )(x_lat, x_ctx, m, m, a_blk, wb_blk, cm_blk, d_skip.reshape(1, d))


def _s5_block_params(ar, ai, br, bi, c_re, c_im):
    _, n_groups, p_dim, gh = br.shape
    gpb = _LANES // gh
    n_blk = n_groups // gpb
    s_dim = gpb * p_dim
    eye = jnp.eye(gpb, dtype=f32)
    a_blk = jnp.stack([ar.reshape(2, n_blk, s_dim), ai.reshape(2, n_blk, s_dim)], axis=2)

    def b_mat(bx):
        t = bx.reshape(2, n_blk, gpb, p_dim, gh)
        return jnp.einsum("dngph,gk->dnghkp", t, eye).reshape(2, n_blk, gpb * gh, s_dim)

    def c_mat(cx):
        t = cx.reshape(2, n_blk, gpb, gh, p_dim)
        return jnp.einsum("dnghp,gk->dngpkh", t, eye).reshape(2, n_blk, s_dim, gpb * gh)

    wb = jnp.concatenate([b_mat(br), b_mat(bi)], axis=-1).astype(bf16)
    cm = jnp.concatenate([c_mat(c_re), -c_mat(c_im)], axis=-2).astype(bf16)
    return a_blk, wb, cm


def _moe_layer(h2_list, logits_list, dims, w_gate, w_up, w_down, layer, *, batch):
    idxs, gates, caps = [], [], []
    for lg, n in zip(logits_list, dims):
        idx, gate = _route(lg, batch=batch, n=n)
        idxs.append(idx)
        gates.append(gate)
        caps.append(idx.shape[-1])
    n_exp = w_gate.shape[1]
    idx_all = jnp.concatenate([jnp.swapaxes(i, 0, 1).reshape(n_exp, -1) for i in idxs], axis=1)
    gate_all = jnp.concatenate([jnp.swapaxes(g, 0, 1).reshape(n_exp, -1) for g in gates], axis=1)
    seg_counts = [batch * c for c in caps]
    idx_flat = idx_all.reshape(-1)
    ye = _moe_experts(idx_flat, h2_list, seg_counts, gate_all[..., None], w_gate, w_up, w_down,
                      layer)
    outs, slot_base = [], 0
    for n, cap in zip(dims, caps):
        outs.append(_combine(idx_flat, ye, batch=batch, n=n, cap=cap, slot_base=slot_base))
        slot_base += batch * cap
    return outs


def _forward(x, c, ctx, c_ctx, w_mod, b_mod, ln_g, ln_b, na_w_qkv, na_w_o, na_rpb, s5_lam_re,
             s5_lam_im, s5_log_step, s5_b_re, s5_b_im, s5_c_re, s5_c_im, s5_d, s5_w_val, s5_w_gate,
             moe_w_router, moe_w_gate, moe_w_up, moe_w_down, *, n_heads):
    batch, seq, d = x.shape
    ctx_len = ctx.shape[1]
    depth = w_mod.shape[0]
    assert depth == 2
    alpha = (2 * depth) ** 0.25

    cond = jnp.concatenate([c, c_ctx[None, :]], axis=0)
    m = _modulation(cond, w_mod, b_mod)
    x_lat = x.reshape(batch * seq, d)
    x_ctx = ctx.reshape(batch * ctx_len, d)
    lat = dict(rows_per_mod=seq, row0=0)
    cx = dict(rows_per_mod=batch * ctx_len, row0=batch)

    w_qkv = na_w_qkv[0].astype(bf16)
    w_o = na_w_o[0].astype(bf16)
    qkv_lat = _mod_proj(x_lat, m, 0, w_qkv, **lat)
    qkv_ctx = _mod_proj(x_ctx, m, 0, w_qkv, **cx)
    o_lat, o_ctx = _attention(qkv_lat, qkv_ctx, na_rpb[0], batch=batch, seq=seq, ctx_len=ctx_len,
                              n_heads=n_heads)
    wr_t = moe_w_router[0].T
    x1_lat, h2_lat, lg_lat = _mixer_post(o_lat, [w_o], x_lat, m, 0, ln_g[0, 0], ln_b[0, 0], wr_t,
                                         alpha=alpha, **lat)
    x1_ctx, h2_ctx, lg_ctx = _mixer_post(o_ctx, [w_o], x_ctx, m, 0, ln_g[0, 0], ln_b[0, 0], wr_t,
                                         alpha=alpha, **cx)
    f_lat, f_ctx = _moe_layer([h2_lat, h2_ctx], [lg_lat, lg_ctx], [seq, ctx_len], moe_w_gate,
                              moe_w_up, moe_w_down, 0, batch=batch)
    x_lat = _ffn_post(x1_lat, f_lat, m, 0, ln_g[0, 1], ln_b[0, 1], alpha=alpha, **lat)
    x_ctx = _ffn_post(x1_ctx, f_ctx, m, 0, ln_g[0, 1], ln_b[0, 1], alpha=alpha, **cx)

    ar, ai, br, bi = _s5_discretize(s5_lam_re[0], s5_lam_im[0], s5_log_step[0], s5_b_re[0],
                                    s5_b_im[0])
    a_blk, wb_blk, cm_blk = _s5_block_params(ar, ai, br, bi, s5_c_re[0], s5_c_im[0])
    z = _s5_core(x_lat, x_ctx, m, 1, a_blk, wb_blk, cm_blk, s5_d[0], batch=batch, n=seq,
                 nc=ctx_len)
    x1_lat, h2_lat, lg_lat = _mixer_post(z, [s5_w_val[0].astype(bf16), s5_w_gate[0].astype(bf16)],
                                         x_lat, m, 1, ln_g[1, 0], ln_b[1, 0], moe_w_router[1].T,
                                         alpha=alpha, **lat)
    (f_lat,) = _moe_layer([h2_lat], [lg_lat], [seq], moe_w_gate, moe_w_up, moe_w_down, 1,
                          batch=batch)
    x_lat = _ffn_post(x1_lat, f_lat, m, 1, ln_g[1, 1], ln_b[1, 1], alpha=alpha, **lat)
    return x_lat.reshape(batch, seq, d)


def kernel(x, c, ctx, c_ctx, w_mod, b_mod, ln_g, ln_b, na_w_qkv, na_w_o, na_rpb, s5_lam_re,
           s5_lam_im, s5_log_step, s5_b_re, s5_b_im, s5_c_re, s5_c_im, s5_d, s5_w_val, s5_w_gate,
           moe_w_router, moe_w_gate, moe_w_up, moe_w_down):
    return _forward(x, c, ctx, c_ctx, w_mod, b_mod, ln_g, ln_b, na_w_qkv, na_w_o, na_rpb,
                    s5_lam_re, s5_lam_im, s5_log_step, s5_b_re, s5_b_im, s5_c_re, s5_c_im, s5_d,
                    s5_w_val, s5_w_gate, moe_w_router, moe_w_gate, moe_w_up, moe_w_down,
                    n_heads=_N_HEADS)
```

```python
import functools
import math

import jax
import jax.numpy as jnp
from jax import lax
from jax.experimental import pallas as pl
from jax.experimental.pallas import tpu as pltpu

f32 = jnp.float32
bf16 = jnp.bfloat16
i32 = jnp.int32

_GRID_W = 64
_WIN_R = 8
_WIN_C = 16
_N_HEADS = 16
_SSM_GROUP = 16
_CAPACITY_FACTOR = 2
_LN_EPS = 1e-5
_NEG_INF = -1e30

_LANES = 128
_SUBLANES = 8
_VMEM_LIMIT_BYTES = 56 * 1024 * 1024

_NT_DIMS = (((1,), (1,)), ((), ()))


def _cparams(sem):
    return pltpu.CompilerParams(dimension_semantics=sem, vmem_limit_bytes=_VMEM_LIMIT_BYTES)


_MOD_UNROLL = 4


def _mod_body(cb_ref, w_ref, b_ref, o_ref, s_scr, *, n_rows):
    d, tn = w_ref.shape
    reps = tn // _LANES

    @pl.when((pl.program_id(0) == 0) & (pl.program_id(1) == 0))
    def _():
        cv = cb_ref[...]
        s_scr[...] = cv * jax.nn.sigmoid(cv)

    def step(i, accs):
        accs = list(accs)
        for v in range(_MOD_UNROLL):
            k0 = pl.multiple_of((i * _MOD_UNROLL + v) * _SUBLANES, _SUBLANES)
            w = w_ref[pl.ds(k0, _SUBLANES), :]
            for r in range(n_rows):
                s = s_scr[r, pl.ds(k0, _SUBLANES), :]
                accs[r] = accs[r] + jnp.tile(s, (1, reps)) * w
        return tuple(accs)

    init = tuple(jnp.zeros((_SUBLANES, tn), f32) for _ in range(n_rows))
    accs = lax.fori_loop(0, d // (_SUBLANES * _MOD_UNROLL), step, init)
    rows = [jnp.sum(a, axis=0, keepdims=True) + b_ref[...] for a in accs]
    rows.append(jnp.zeros((_SUBLANES - n_rows, tn), f32))
    o_ref[...] = jnp.concatenate(rows, axis=0)


def _modulation(cond, w_mod, b_mod):
    n_rows, d = cond.shape
    depth, _, n6 = w_mod.shape
    tn = min(1024, n6)
    cb = jnp.broadcast_to(cond[:, :, None], (n_rows, d, _LANES))
    return pl.pallas_call(
        functools.partial(_mod_body, n_rows=n_rows),
        grid=(depth, n6 // tn),
        in_specs=[
            pl.BlockSpec((n_rows, d, _LANES), lambda l, j: (0, 0, 0)),
            pl.BlockSpec((None, d, tn), lambda l, j: (l, 0, j)),
            pl.BlockSpec((None, 1, tn), lambda l, j: (l, 0, j)),
        ],
        out_specs=pl.BlockSpec((None, _SUBLANES, tn), lambda l, j: (l, 0, j)),
        out_shape=jax.ShapeDtypeStruct((depth, _SUBLANES, n6), f32),
        scratch_shapes=[pltpu.VMEM((n_rows, d, _LANES), f32)],
        compiler_params=_cparams(("arbitrary", "arbitrary")),
        name="modulation",
    )(cb, w_mod, b_mod.reshape(depth, 1, n6))


def _mod_spec(layer, chunk, d, nargs):
    if nargs == 1:
        return pl.BlockSpec((None, _SUBLANES, d), lambda i: (layer, 0, chunk))
    return pl.BlockSpec((None, _SUBLANES, d), lambda i, j: (layer, 0, chunk))


def _proj_body(x_ref, sc_ref, sh_ref, w_ref, o_ref, h_scr, *, tiles_per_row, row0):
    @pl.when(pl.program_id(1) == 0)
    def _():
        r = row0 + pl.program_id(0) // tiles_per_row
        sc = sc_ref[pl.ds(r, 1), :]
        sh = sh_ref[pl.ds(r, 1), :]
        h_scr[...] = (x_ref[...] * (1.0 + sc) + sh).astype(bf16)

    o_ref[...] = jnp.dot(h_scr[...], w_ref[...], preferred_element_type=f32).astype(o_ref.dtype)


def _mod_proj(x, m, layer, w_bf, *, rows_per_mod, row0):
    mtot, d = x.shape
    n = w_bf.shape[1]
    tm = min(1024, rows_per_mod, mtot)
    tn = min(512, n)
    return pl.pallas_call(
        functools.partial(_proj_body, tiles_per_row=rows_per_mod // tm, row0=row0),
        grid=(mtot // tm, n // tn),
        in_specs=[
            pl.BlockSpec((tm, d), lambda i, j: (i, 0)),
            _mod_spec(layer, 1, d, 2),
            _mod_spec(layer, 0, d, 2),
            pl.BlockSpec((d, tn), lambda i, j: (0, j)),
        ],
        out_specs=pl.BlockSpec((tm, tn), lambda i, j: (i, j)),
        out_shape=jax.ShapeDtypeStruct((mtot, n), bf16),
        scratch_shapes=[pltpu.VMEM((tm, d), bf16)],
        compiler_params=_cparams(("arbitrary", "arbitrary")),
        name="mod_proj",
    )(x, m, m, w_bf)


_ATTN_PAIR_UNROLL = 4
_ATTN_VT_UNROLL = 4
_ATTN_KEY_ROWS = _WIN_R + 2


def _attn_body(rpb_ref, q_ref, k_ref, v_ref, qc_ref, kc_ref, vc_ref, o_ref, oc_ref, t_scr, vt_scr,
               *, rows, dh):
    w = _GRID_W
    n_dr = 2 * _WIN_R - 1
    n_dc = 2 * _WIN_C - 1
    masked = n_dr
    h = pl.program_id(0)
    scale = dh ** -0.5
    lane = lax.broadcasted_iota(i32, (w, 2 * w), 1)
    first_row = lane < w

    @pl.when(pl.program_id(1) == 0)
    def _build():
        kc = lax.broadcasted_iota(i32, (w, 2 * w), 0)
        c = lane & (w - 1)
        delta = jnp.clip(kc - c + (_WIN_C - 1), 0, n_dc - 1)
        cstart = jnp.clip(c - _WIN_C // 2, 0, w - _WIN_C)
        valid = (kc >= cstart) & (kc < cstart + _WIN_C)
        base = h * (n_dr * n_dc)

        def dr_body(dr, carry):
            acc = jnp.zeros((w, 2 * w), f32)
            for d in range(n_dc):
                acc = jnp.where(delta == d, rpb_ref[base + dr * n_dc + d], acc)
            t_scr[dr] = jnp.where(valid, acc, _NEG_INF)
            return carry

        lax.fori_loop(0, n_dr, dr_body, 0)
        t_scr[masked] = jnp.full((w, 2 * w), _NEG_INF, f32)

    def vt_body(i, carry):
        for u in range(_ATTN_VT_UNROLL):
            blk_i = i * _ATTN_VT_UNROLL + u
            blk = v_ref[pl.ds(pl.multiple_of(blk_i * 2 * w, 2 * w), 2 * w), :]
            vt_scr[blk_i] = blk.astype(f32).T.astype(bf16)
        return carry

    lax.fori_loop(0, rows // (2 * _ATTN_VT_UNROLL), vt_body, 0)
    kcx = kc_ref[...]
    vcx = vc_ref[...]
    vct = vcx.astype(f32).T.astype(bf16)

    def pair_scores(i):
        r = 2 * i
        band = jnp.minimum(jnp.clip(r - _WIN_R // 2, 0, rows - _WIN_R - 1) & -2,
                           rows - _ATTN_KEY_ROWS)
        q0 = pl.multiple_of(r * w, 2 * w)
        k0 = pl.multiple_of(band * w, 2 * w)
        q2 = q_ref[pl.ds(q0, 2 * w), :]
        kw = k_ref[pl.ds(k0, _ATTN_KEY_ROWS * w), :]
        blocks = []
        drs = []
        for u in range(2):
            rs = jnp.clip(r + u - _WIN_R // 2, 0, rows - _WIN_R)
            drs.append((rs - (r + u) + (_WIN_R - 1), rs - band))
        for kr in range(_ATTN_KEY_ROWS):
            idx = []
            for s0, off in drs:
                rel = kr - off
                idx.append(jnp.where((rel >= 0) & (rel < _WIN_R), s0 + rel, masked))
            blocks.append(jnp.where(first_row, t_scr[idx[0]], t_scr[idx[1]]))
        bias = jnp.concatenate(blocks, axis=0)
        s_loc = lax.dot_general(kw, q2, _NT_DIMS, preferred_element_type=f32) * scale + bias
        s_ctx = lax.dot_general(kcx, q2, _NT_DIMS, preferred_element_type=f32) * scale
        return s_loc, s_ctx, band, q0

    def pair_output(s_loc, s_ctx, band, q0):
        mx = jnp.maximum(jnp.max(s_loc, axis=0, keepdims=True),
                         jnp.max(s_ctx, axis=0, keepdims=True))
        p_loc = jnp.exp(s_loc - mx)
        p_ctx = jnp.exp(s_ctx - mx)
        den = jnp.sum(p_loc, axis=0, keepdims=True) + jnp.sum(p_ctx, axis=0, keepdims=True)
        vb = band // 2
        vwt = jnp.concatenate([vt_scr[vb + t] for t in range(_ATTN_KEY_ROWS // 2)], axis=1)
        o_t = (jnp.dot(vwt, p_loc.astype(bf16), preferred_element_type=f32)
               + jnp.dot(vct, p_ctx.astype(bf16), preferred_element_type=f32))
        o_ref[pl.ds(q0, 2 * w), :] = (o_t / den).T.astype(o_ref.dtype)

    def pair_body(i, carry):
        scores = [pair_scores(i * _ATTN_PAIR_UNROLL + u) for u in range(_ATTN_PAIR_UNROLL)]
        for sc in scores:
            pair_output(*sc)
        return carry

    lax.fori_loop(0, rows // (2 * _ATTN_PAIR_UNROLL), pair_body, 0)

    s = lax.dot_general(qc_ref[...], kcx, _NT_DIMS, preferred_element_type=f32) * scale
    p = jnp.exp(s - jnp.max(s, axis=1, keepdims=True))
    den = jnp.sum(p, axis=1, keepdims=True)
    oc = jnp.dot(p.astype(bf16), vcx, preferred_element_type=f32)
    oc_ref[...] = (oc / den).astype(oc_ref.dtype)


def _attention(qkv_lat, qkv_ctx, rpb, *, batch, seq, ctx_len, n_heads):
    d = qkv_lat.shape[1] // 3
    dh = d // n_heads
    rows = seq // _GRID_W
    assert dh == _LANES and _GRID_W * 2 == _LANES and rows >= _WIN_R
    assert rows % (2 * _ATTN_PAIR_UNROLL) == 0 and rows >= _ATTN_KEY_ROWS
    assert rows % (2 * _ATTN_VT_UNROLL) == 0
    n_dr = 2 * _WIN_R - 1
    return pl.pallas_call(
        functools.partial(_attn_body, rows=rows, dh=dh),
        grid=(n_heads, batch),
        in_specs=[
            pl.BlockSpec(memory_space=pltpu.SMEM),
            pl.BlockSpec((seq, dh), lambda h, b: (b, h)),
            pl.BlockSpec((seq, dh), lambda h, b: (b, n_heads + h)),
            pl.BlockSpec((seq, dh), lambda h, b: (b, 2 * n_heads + h)),
            pl.BlockSpec((ctx_len, dh), lambda h, b: (b, h)),
            pl.BlockSpec((ctx_len, dh), lambda h, b: (b, n_heads + h)),
            pl.BlockSpec((ctx_len, dh), lambda h, b: (b, 2 * n_heads + h)),
        ],
        out_specs=[
            pl.BlockSpec((seq, dh), lambda h, b: (b, h)),
            pl.BlockSpec((ctx_len, dh), lambda h, b: (b, h)),
        ],
        out_shape=[
            jax.ShapeDtypeStruct((batch * seq, d), bf16),
            jax.ShapeDtypeStruct((batch * ctx_len, d), bf16),
        ],
        scratch_shapes=[
            pltpu.VMEM((n_dr + 1, _GRID_W, 2 * _GRID_W), f32),
            pltpu.VMEM((rows // 2, dh, 2 * _GRID_W), bf16),
        ],
        compiler_params=_cparams(("arbitrary", "arbitrary")),
        name="nbr_attention",
    )(rpb.reshape(-1), qkv_lat, qkv_lat, qkv_lat, qkv_ctx, qkv_ctx, qkv_ctx)


def _layer_norm(y, g, b):
    mu = jnp.mean(y, axis=-1, keepdims=True)
    yc = y - mu
    var = jnp.mean(yc * yc, axis=-1, keepdims=True)
    return yc * lax.rsqrt(var + _LN_EPS) * g + b


def _post_body(*refs, glu, alpha, tiles_per_row, row0):
    if glu:
        a_ref, wv_ref, wg_ref = refs[:3]
        rest = refs[3:]
    else:
        a_ref, wv_ref = refs[:2]
        rest = refs[2:]
    x_ref, g1_ref, sc2_ref, sh2_ref, lng_ref, lnb_ref, wr_ref, x1_ref, h2_ref, lg_ref = rest
    a = a_ref[...]
    o = jnp.dot(a, wv_ref[...], preferred_element_type=f32)
    if glu:
        o = o * jax.nn.sigmoid(jnp.dot(a, wg_ref[...], preferred_element_type=f32))
    r = row0 + pl.program_id(0) // tiles_per_row
    g1 = g1_ref[pl.ds(r, 1), :]
    x1 = _layer_norm(alpha * x_ref[...] + g1 * o, lng_ref[...], lnb_ref[...])
    h2 = x1 * (1.0 + sc2_ref[pl.ds(r, 1), :]) + sh2_ref[pl.ds(r, 1), :]
    x1_ref[...] = x1
    h2_ref[...] = _pack_bf16_pairs(h2)
    n_exp = wr_ref.shape[0]
    wr = wr_ref[...]
    w_hi = wr.astype(bf16)
    w_lo = (wr - w_hi.astype(f32)).astype(bf16)
    h_hi = h2.astype(bf16)
    h_lo = (h2 - h_hi.astype(f32)).astype(bf16)
    both = lax.dot_general(jnp.concatenate([w_hi, w_lo], axis=0), h_hi, _NT_DIMS,
                           preferred_element_type=f32)
    lg_ref[...] = (both[:n_exp] + both[n_exp:]
                   + lax.dot_general(w_hi, h_lo, _NT_DIMS, preferred_element_type=f32))


def _mixer_post(a_bf, weights_bf, x, m, layer, ln_g, ln_b, w_router_t, *, alpha, rows_per_mod,
                row0):
    mtot, d = x.shape
    n_exp = w_router_t.shape[0]
    glu = len(weights_bf) == 2
    tm = min(256 if glu else 512, mtot, rows_per_mod)
    const = lambda i: (0, 0)
    in_specs = [pl.BlockSpec((tm, d), lambda i: (i, 0))]
    in_specs += [pl.BlockSpec((d, d), const, pipeline_mode=pl.Buffered(1)) for _ in weights_bf]
    in_specs += [
        pl.BlockSpec((tm, d), lambda i: (i, 0)),
        _mod_spec(layer, 2, d, 1),
        _mod_spec(layer, 4, d, 1),
        _mod_spec(layer, 3, d, 1),
        pl.BlockSpec((1, d), const),
        pl.BlockSpec((1, d), const),
        pl.BlockSpec((n_exp, d), const),
    ]
    return pl.pallas_call(
        functools.partial(_post_body, glu=glu, alpha=alpha, tiles_per_row=rows_per_mod // tm,
                          row0=row0),
        grid=(mtot // tm,),
        in_specs=in_specs,
        out_specs=[
            pl.BlockSpec((tm, d), lambda i: (i, 0)),
            pl.BlockSpec((tm, d // 2), lambda i: (i, 0)),
            pl.BlockSpec((n_exp, tm), lambda i: (0, i)),
        ],
        out_shape=[
            jax.ShapeDtypeStruct((mtot, d), f32),
            jax.ShapeDtypeStruct((mtot, d // 2), jnp.uint32),
            jax.ShapeDtypeStruct((n_exp, mtot), f32),
        ],
        compiler_params=_cparams(("arbitrary",)),
        name="mixer_post_glu" if glu else "mixer_post",
    )(a_bf, *weights_bf, x, m, m, m, ln_g.reshape(1, d), ln_b.reshape(1, d), w_router_t)


def _cumsum_excl(mask01):
    rows, n = mask01.shape
    r = lax.broadcasted_iota(i32, (_LANES, _LANES), 0)
    c = lax.broadcasted_iota(i32, (_LANES, _LANES), 1)
    tri = jnp.where(r < c, 1.0, 0.0).astype(bf16)
    carry = jnp.zeros((rows, 1), f32)
    outs = []
    for t in range(n // _LANES):
        blk = mask01[:, t * _LANES:(t + 1) * _LANES]
        outs.append(jnp.dot(blk.astype(bf16), tri, preferred_element_type=f32) + carry)
        carry = carry + jnp.sum(blk, axis=1, keepdims=True)
    return outs[0] if len(outs) == 1 else jnp.concatenate(outs, axis=1)


def _route_body(lg_ref, idx_ref, gate_ref, posm_scr, aff_scr, *, n, cap, chunk):
    n_exp = lg_ref.shape[0]
    x = lg_ref[...]
    ex = jnp.exp(x - jnp.max(x, axis=0, keepdims=True))
    aff = ex / jnp.sum(ex, axis=0, keepdims=True)
    bits = pltpu.bitcast(aff, i32)

    thr = jnp.zeros((n_exp, 1), i32)
    for bit in range(30, -1, -1):
        cand = thr | (1 << bit)
        cnt = jnp.sum(jnp.where(bits >= cand, 1.0, 0.0), axis=1, keepdims=True)
        thr = jnp.where(cnt >= cap, cand, thr)
    gt = bits > thr
    eq = bits == thr
    need = cap - jnp.sum(jnp.where(gt, 1.0, 0.0), axis=1, keepdims=True)
    rank_eq = _cumsum_excl(jnp.where(eq, 1.0, 0.0))
    sel = gt | (eq & (rank_eq < need))
    pos = _cumsum_excl(jnp.where(sel, 1.0, 0.0))
    posm_scr[...] = jnp.where(sel, pos, -1.0)
    a_hi = aff.astype(bf16)
    r1 = aff - a_hi.astype(f32)
    a_mid = r1.astype(bf16)
    a_lo = (r1 - a_mid.astype(f32)).astype(bf16)
    for k, piece in enumerate((a_hi, a_mid, a_lo)):
        aff_scr[k] = piece.astype(f32)
    tok = lax.broadcasted_iota(i32, (1, n), 1)
    digits = jnp.concatenate([(tok >> 6).astype(f32), (tok & 63).astype(f32)], axis=0)
    pad = jnp.zeros((_SUBLANES - 5, n), f32)
    tok_base = pl.program_id(0) * n

    def per_expert(e, carry):
        prow = posm_scr[pl.ds(e, 1), :]
        rows = [digits] + [aff_scr[k, pl.ds(e, 1), :] for k in range(3)] + [pad]
        vals = jnp.concatenate(rows, axis=0).astype(bf16)
        outs = []
        for pc in range(cap // chunk):
            slot = (lax.broadcasted_iota(i32, (chunk, n), 0) + pc * chunk).astype(f32)
            hit = jnp.where(prow == slot, 1.0, 0.0).astype(bf16)
            outs.append(lax.dot_general(vals, hit, _NT_DIMS, preferred_element_type=f32))
        got = outs[0] if len(outs) == 1 else jnp.concatenate(outs, axis=1)
        idx = got[0:1] * 64.0 + got[1:2]
        idx_ref[pl.ds(e, 1), :] = idx.astype(i32) + tok_base
        gate_ref[pl.ds(e, 1), :] = got[2:3] + got[3:4] + got[4:5]
        return carry

    lax.fori_loop(0, n_exp, per_expert, 0)


def _route(logits_t, *, batch, n):
    n_exp = logits_t.shape[0]
    cap = _CAPACITY_FACTOR * n // n_exp
    chunk = min(_LANES, cap)
    assert n % _LANES == 0 and cap % chunk == 0 and n <= 64 * 64
    return pl.pallas_call(
        functools.partial(_route_body, n=n, cap=cap, chunk=chunk),
        grid=(batch,),
        in_specs=[pl.BlockSpec((n_exp, n), lambda b: (0, b))],
        out_specs=[
            pl.BlockSpec((None, n_exp, cap), lambda b: (b, 0, 0)),
            pl.BlockSpec((None, n_exp, cap), lambda b: (b, 0, 0)),
        ],
        out_shape=[
            jax.ShapeDtypeStruct((batch, n_exp, cap), i32),
            jax.ShapeDtypeStruct((batch, n_exp, cap), f32),
        ],
        scratch_shapes=[pltpu.VMEM((n_exp, n), f32), pltpu.VMEM((3, n_exp, n), f32)],
        compiler_params=_cparams(("arbitrary",)),
        name="route_topk",
    )(logits_t)


def _pack_bf16_pairs(h):
    half = h.shape[1] // 2
    u = pltpu.bitcast(h.astype(bf16).astype(f32), jnp.uint32)
    return (u[:, :half] & jnp.uint32(0xFFFF0000)) | (u[:, half:] >> 16)


def _unpack_bf16_pairs(p):
    hi = pltpu.bitcast(p & jnp.uint32(0xFFFF0000), f32).astype(bf16)
    lo = pltpu.bitcast(p << 16, f32).astype(bf16)
    return jnp.concatenate([hi, lo], axis=1)


def _moe_body(idx_ref, *refs, seg_counts, m_slots, n_tiles):
    n_src = len(seg_counts)
    srcs = refs[:n_src]
    gate_ref, wg_ref, wu_ref, wd_ref, o_ref, xp_scr, xe16, hmid, sem = refs[n_src:]
    e = pl.program_id(0)
    j = pl.program_id(1)
    tf = wg_ref.shape[1]

    def row_copy(src, row, blk, sub):
        return pltpu.make_async_copy(src.at[pl.ds(row, 1), :], xp_scr.at[blk, pl.ds(sub, 1), :],
                                     sem)

    def for_each_slot(fn):
        slot0 = 0
        for src, cnt in zip(srcs, seg_counts):
            def body(i, c, src=src, slot0=slot0):
                for u in range(_SUBLANES):
                    fn(src, slot0 // _SUBLANES + i, u)
                return c

            lax.fori_loop(0, cnt // _SUBLANES, body, 0)
            slot0 += cnt

    def issue_gather(expert):
        base = expert * m_slots
        for_each_slot(lambda src, blk, sub: row_copy(
            src, idx_ref[base + blk * _SUBLANES + sub], blk, sub).start())

    @pl.when(j == 0)
    def _rows():
        @pl.when(e == 0)
        def _():
            issue_gather(0)

        for_each_slot(lambda src, blk, sub: row_copy(src, 0, blk, sub).wait())
        xe16[...] = _unpack_bf16_pairs(xp_scr[...].reshape(m_slots, xp_scr.shape[-1]))

        @pl.when(e + 1 < pl.num_programs(0))
        def _():
            issue_gather(e + 1)

    @pl.when(j < n_tiles)
    def _up():
        x = xe16[...]
        a = jnp.dot(x, wg_ref[...].astype(bf16), preferred_element_type=f32)
        u = jnp.dot(x, wu_ref[...].astype(bf16), preferred_element_type=f32)
        hmid[j] = (a * jax.nn.sigmoid(a) * u).astype(bf16)

    @pl.when(j >= n_tiles)
    def _down():
        y = jnp.dot(hmid[0], wd_ref[0:tf, :].astype(bf16), preferred_element_type=f32)
        for k in range(1, n_tiles):
            y += jnp.dot(hmid[k], wd_ref[k * tf:(k + 1) * tf, :].astype(bf16),
                         preferred_element_type=f32)
        o_ref[...] = y * gate_ref[...]


def _moe_experts(idx_flat, srcs, seg_counts, gates, w_gate, w_up, w_down, layer):
    _, n_exp, d, ff = w_gate.shape
    m_slots = sum(seg_counts)
    tf = min(512, ff)
    n_tiles = ff // tf
    assert d // tf == n_tiles and all(c % _SUBLANES == 0 for c in seg_counts)
    n_src = len(srcs)
    up_idx = lambda e, j, idx: (layer, e, 0, jnp.minimum(j, n_tiles - 1))
    dn_idx = lambda e, j, idx: (layer, e, 0, jnp.maximum(j - n_tiles, 0))
    grid_spec = pltpu.PrefetchScalarGridSpec(
        num_scalar_prefetch=1,
        grid=(n_exp, 2 * n_tiles),
        in_specs=[pl.BlockSpec(memory_space=pl.ANY)] * n_src + [
            pl.BlockSpec((None, m_slots, 1), lambda e, j, idx: (e, 0, 0)),
            pl.BlockSpec((None, None, d, tf), up_idx),
            pl.BlockSpec((None, None, d, tf), up_idx),
            pl.BlockSpec((None, None, ff, tf), dn_idx),
        ],
        out_specs=pl.BlockSpec((None, m_slots, tf),
                               lambda e, j, idx: (e, 0, jnp.maximum(j - n_tiles, 0))),
        scratch_shapes=[
            pltpu.VMEM((m_slots // _SUBLANES, _SUBLANES, d // 2), jnp.uint32),
            pltpu.VMEM((m_slots, d), bf16),
            pltpu.VMEM((n_tiles, m_slots, tf), bf16),
            pltpu.SemaphoreType.DMA(()),
        ],
    )
    return pl.pallas_call(
        functools.partial(_moe_body, seg_counts=tuple(seg_counts), m_slots=m_slots,
                          n_tiles=n_tiles),
        grid_spec=grid_spec,
        out_shape=jax.ShapeDtypeStruct((n_exp, m_slots, d), f32),
        compiler_params=_cparams(("arbitrary", "arbitrary")),
        name="moe_experts",
    )(idx_flat, *srcs, gates, w_gate, w_up, w_down)


_SCATTER_UNROLL = 2


def _combine_body(idx_ref, ye_ref, o_hbm, acc, sem, *, m_slots, slot_base, cap, n):
    b = pl.program_id(0)
    e = pl.program_id(1)

    @pl.when(e == 0)
    def _():
        acc[...] = jnp.zeros_like(acc)

    base = e * m_slots + slot_base + b * cap
    tok0 = b * n

    def body(i, carry):
        j0 = i * _SCATTER_UNROLL
        toks = [idx_ref[base + j0 + u] - tok0 for u in range(_SCATTER_UNROLL)]
        vals = [acc[pl.ds(toks[u], 1), :] + ye_ref[pl.ds(j0 + u, 1), :]
                for u in range(_SCATTER_UNROLL)]
        for u in range(_SCATTER_UNROLL):
            acc[pl.ds(toks[u], 1), :] = vals[u]
        return carry

    lax.fori_loop(0, cap // _SCATTER_UNROLL, body, 0)

    @pl.when(e == pl.num_programs(1) - 1)
    def _():
        out = pltpu.make_async_copy(acc, o_hbm.at[pl.ds(pl.multiple_of(b * n, n), n), :], sem)
        out.start()
        out.wait()


def _combine(idx_flat, ye, *, batch, n, cap, slot_base):
    n_exp, m_slots, d = ye.shape
    assert slot_base % cap == 0 and cap % _SCATTER_UNROLL == 0
    blk0 = slot_base // cap
    grid_spec = pltpu.PrefetchScalarGridSpec(
        num_scalar_prefetch=1,
        grid=(batch, n_exp),
        in_specs=[pl.BlockSpec((None, cap, d), lambda b, e, idx: (e, blk0 + b, 0))],
        out_specs=pl.BlockSpec(memory_space=pl.ANY),
        scratch_shapes=[pltpu.VMEM((n, d), f32), pltpu.SemaphoreType.DMA(())],
    )
    return pl.pallas_call(
        functools.partial(_combine_body, m_slots=m_slots, slot_base=slot_base, cap=cap, n=n),
        grid_spec=grid_spec,
        out_shape=jax.ShapeDtypeStruct((batch * n, d), f32),
        compiler_params=_cparams(("arbitrary", "arbitrary")),
        name="moe_combine",
    )(idx_flat, ye)


def _ffn_post_body(x_ref, f_ref, g2_ref, lng_ref, lnb_ref, o_ref, *, alpha, tiles_per_row, row0):
    r = row0 + pl.program_id(0) // tiles_per_row
    g2 = g2_ref[pl.ds(r, 1), :]
    o_ref[...] = _layer_norm(alpha * x_ref[...] + g2 * f_ref[...], lng_ref[...], lnb_ref[...])


def _ffn_post(x1, fo, m, layer, ln_g, ln_b, *, alpha, rows_per_mod, row0):
    mtot, d = x1.shape
    tm = min(512, mtot, rows_per_mod)
    const = lambda i: (0, 0)
    return pl.pallas_call(
        functools.partial(_ffn_post_body, alpha=alpha, tiles_per_row=rows_per_mod // tm, row0=row0),
        grid=(mtot // tm,),
        in_specs=[
            pl.BlockSpec((tm, d), lambda i: (i, 0)),
            pl.BlockSpec((tm, d), lambda i: (i, 0)),
            _mod_spec(layer, 5, d, 1),
            pl.BlockSpec((1, d), const),
            pl.BlockSpec((1, d), const),
        ],
        out_specs=pl.BlockSpec((tm, d), lambda i: (i, 0)),
        out_shape=jax.ShapeDtypeStruct((mtot, d), f32),
        compiler_params=_cparams(("arbitrary",)),
        name="ffn_post",
    )(x1, fo, m, ln_g.reshape(1, d), ln_b.reshape(1, d))


def _disc_body(lr_ref, li_ref, ls_ref, bre_ref, bim_ref, ar_ref, ai_ref, br_ref, bi_ref):
    lr = jnp.minimum(lr_ref[...], -1e-4)
    li = li_ref[...]
    dt = jnp.exp(ls_ref[...])
    mag = jnp.exp(lr * dt)
    ar = mag * jnp.cos(li * dt)
    ai = mag * jnp.sin(li * dt)
    nr = ar - 1.0
    den = lr * lr + li * li
    cr = (nr * lr + ai * li) / den
    ci = (ai * lr - nr * li) / den
    ar_ref[...] = ar
    ai_ref[...] = ai
    br_ref[...] = cr * bre_ref[...] - ci * bim_ref[...]
    bi_ref[...] = cr * bim_ref[...] + ci * bre_ref[...]


def _s5_discretize(lam_re, lam_im, log_step, b_re, b_im):
    shape = b_re.shape
    flat = (shape[0] * shape[1], shape[2] * shape[3])
    bc = lambda a: jnp.broadcast_to(a, shape).reshape(flat)
    outs = pl.pallas_call(
        _disc_body,
        out_shape=[jax.ShapeDtypeStruct(flat, f32)] * 4,
        name="s5_discretize",
    )(bc(lam_re[..., None]), bc(lam_im[..., None]), bc(log_step[:, :, None, None]),
      b_re.reshape(flat), b_im.reshape(flat))
    ar, ai, br, bi = [o.reshape(shape) for o in outs]
    return ar[..., 0], ai[..., 0], br, bi


_SCAN_UNROLL = 8


def _cmul(ar, ai, xr, xi):
    return ar * xr - ai * xi, ar * xi + ai * xr


def _cpow(ar, ai, k):
    rr, ri = jnp.ones_like(ar), jnp.zeros_like(ai)
    while k:
        if k & 1:
            rr, ri = _cmul(ar, ai, rr, ri)
        ar, ai = _cmul(ar, ai, ar, ai)
        k >>= 1
    return rr, ri


def _segment_scan(bu_ref, seg_len, s_dim, ar, ai, init, reverse, store):
    ns = _SUBLANES
    unroll = math.gcd(seg_len, _SCAN_UNROLL)

    def outer(jo, carry):
        xr, xi = carry
        for ji in range(unroll):
            jj = jo * unroll + ji
            j = (seg_len - 1 - jj) if reverse else jj
            r0 = pl.multiple_of(j * ns, ns)
            br = bu_ref[pl.ds(r0, ns), 0:s_dim]
            bi = bu_ref[pl.ds(r0, ns), s_dim:2 * s_dim]
            xr, xi = ar * xr - ai * xi + br, ar * xi + ai * xr + bi
            if store:
                bu_ref[pl.ds(r0, ns), 0:s_dim] = xr
                bu_ref[pl.ds(r0, ns), s_dim:2 * s_dim] = xi
        return xr, xi

    return lax.fori_loop(0, seg_len // unroll, outer, init)


def _segment_inits(ends, a_len, h0, reverse):
    er, ei = ends
    alr, ali = a_len
    ns = _SUBLANES
    order = range(ns - 1, -1, -1) if reverse else range(ns)
    cr, ci = h0
    inits_r, inits_i = [None] * ns, [None] * ns
    for s in order:
        inits_r[s], inits_i[s] = cr, ci
        pr, pi = _cmul(alr, ali, cr, ci)
        cr, ci = pr + er[s:s + 1, :], pi + ei[s:s + 1, :]
    return (jnp.concatenate(inits_r, axis=0), jnp.concatenate(inits_i, axis=0)), (cr, ci)


def _s5_body(x_ref, xc_ref, sh_ref, sc_ref, a_ref, wb_ref, cm_ref, d_ref, z_ref,
             up_scr, ucp_scr, bu_scr, buc_scr, y_scr, yn_scr, *, n, nc, batch):
    ns = _SUBLANES
    seg, segc = n // ns, nc // ns
    s_dim = a_ref.shape[-1]
    b = pl.program_id(0)
    sc_l, sh_l = sc_ref[pl.ds(b, 1), :], sh_ref[pl.ds(b, 1), :]
    sc_c, sh_c = sc_ref[batch:batch + 1, :], sh_ref[batch:batch + 1, :]

    for j in range(seg):
        up_scr[j * ns:(j + 1) * ns, :] = x_ref[pl.ds(j, ns, stride=seg), :] * (1.0 + sc_l) + sh_l
    for j in range(segc):
        ucp_scr[j * ns:(j + 1) * ns, :] = xc_ref[pl.ds(j, ns, stride=segc), :] * (1.0 + sc_c) + sh_c

    u = up_scr[...]
    y_scr[...] = d_ref[...] * u
    uc_bf = ucp_scr[...].astype(bf16)
    zero = (jnp.zeros((ns, s_dim), f32), jnp.zeros((ns, s_dim), f32))
    u_bf = u.astype(bf16)
    for direction in range(2):
        reverse = direction == 1
        ar1, ai1 = a_ref[direction, 0:1, :], a_ref[direction, 1:2, :]
        ar = jnp.broadcast_to(ar1, (ns, s_dim))
        ai = jnp.broadcast_to(ai1, (ns, s_dim))
        buc_scr[...] = jnp.dot(uc_bf, wb_ref[direction], preferred_element_type=f32)
        ends_c = _segment_scan(buc_scr, segc, s_dim, ar, ai, zero, reverse, store=False)
        zero1 = (jnp.zeros((1, s_dim), f32), jnp.zeros((1, s_dim), f32))
        _, h0 = _segment_inits(ends_c, _cpow(ar1, ai1, segc), zero1, reverse)
        for lo in (0, n // 2):
            bu_scr[lo:lo + n // 2, :] = jnp.dot(u_bf[lo:lo + n // 2], wb_ref[direction],
                                                preferred_element_type=f32)
        ends = _segment_scan(bu_scr, seg, s_dim, ar, ai, zero, reverse, store=False)
        inits, _ = _segment_inits(ends, _cpow(ar1, ai1, seg), h0, reverse)
        _segment_scan(bu_scr, seg, s_dim, ar, ai, inits, reverse, store=True)
        for lo in (0, n // 2):
            y_scr[lo:lo + n // 2, :] += jnp.dot(bu_scr[lo:lo + n // 2, :].astype(bf16),
                                                cm_ref[direction], preferred_element_type=f32)

    for j in range(seg):
        yn_scr[pl.ds(j, ns, stride=seg), :] = y_scr[j * ns:(j + 1) * ns, :]
    z_ref[...] = jax.nn.gelu(yn_scr[...]).astype(z_ref.dtype)


def _s5_core(x_lat, x_ctx, m, layer, a_blk, wb_blk, cm_blk, d_skip, *, batch, n, nc):
    d = x_lat.shape[1]
    n_blk = d // _LANES
    s_dim = a_blk.shape[-1]
    assert n % (_SUBLANES * _SCAN_UNROLL) == 0 and nc % _SUBLANES == 0
    return pl.pallas_call(
        functools.partial(_s5_body, n=n, nc=nc, batch=batch),
        grid=(batch, n_blk),
        in_specs=[
            pl.BlockSpec((n, _LANES), lambda b, g: (b, g)),
            pl.BlockSpec((nc, _LANES), lambda b, g: (b, g)),
            pl.BlockSpec((None, _SUBLANES, _LANES), lambda b, g: (layer, 0, g)),
            pl.BlockSpec((None, _SUBLANES, _LANES), lambda b, g: (layer, 0, n_blk + g)),
            pl.BlockSpec((2, None, 2, s_dim), lambda b, g: (0, g, 0, 0)),
            pl.BlockSpec((2, None, _LANES, 2 * s_dim), lambda b, g: (0, g, 0, 0)),
            pl.BlockSpec((2, None, 2 * s_dim, _LANES), lambda b, g: (0, g, 0, 0)),
            pl.BlockSpec((1, _LANES), lambda b, g: (0, g)),
        ],
        out_specs=pl.BlockSpec((n, _LANES), lambda b, g: (b, g)),
        out_shape=jax.ShapeDtypeStruct((batch * n, d), bf16),
        scratch_shapes=[
            pltpu.VMEM((n, _LANES), f32),
            pltpu.VMEM((nc, _LANES), f32),
            pltpu.VMEM((n, 2 * s_dim), f32),
            pltpu.VMEM((nc, 2 * s_dim), f32),
            pltpu.VMEM((n, _LANES), f32),
            pltpu.VMEM((n, _LANES), f32),
        ],
        compiler_params=_cparams(("arbitrary", "arbitrary")),
        name="s5_core",
    )(x_lat, x_ctx, m, m, a_blk, wb_blk, cm_blk, d_skip.reshape(1, d))


def _s5_block_params(ar, ai, br, bi, c_re, c_im):
    _, n_groups, p_dim, gh = br.shape
    gpb = _LANES // gh
    n_blk = n_groups // gpb
    s_dim = gpb * p_dim
    eye = jnp.eye(gpb, dtype=f32)
    a_blk = jnp.stack([ar.reshape(2, n_blk, s_dim), ai.reshape(2, n_blk, s_dim)], axis=2)

    def b_mat(bx):
        t = bx.reshape(2, n_blk, gpb, p_dim, gh)
        return jnp.einsum("dngph,gk->dnghkp", t, eye).reshape(2, n_blk, gpb * gh, s_dim)

    def c_mat(cx):
        t = cx.reshape(2, n_blk, gpb, gh, p_dim)
        return jnp.einsum("dnghp,gk->dngpkh", t, eye).reshape(2, n_blk, s_dim, gpb * gh)

    wb = jnp.concatenate([b_mat(br), b_mat(bi)], axis=-1).astype(bf16)
    cm = jnp.concatenate([c_mat(c_re), -c_mat(c_im)], axis=-2).astype(bf16)
    return a_blk, wb, cm


def _moe_layer(h2_list, logits_list, dims, w_gate, w_up, w_down, layer, *, batch):
    idxs, gates, caps = [], [], []
    for lg, n in zip(logits_list, dims):
        idx, gate = _route(lg, batch=batch, n=n)
        idxs.append(idx)
        gates.append(gate)
        caps.append(idx.shape[-1])
    n_exp = w_gate.shape[1]
    idx_all = jnp.concatenate([jnp.swapaxes(i, 0, 1).reshape(n_exp, -1) for i in idxs], axis=1)
    gate_all = jnp.concatenate([jnp.swapaxes(g, 0, 1).reshape(n_exp, -1) for g in gates], axis=1)
    seg_counts = [batch * c for c in caps]
    idx_flat = idx_all.reshape(-1)
    ye = _moe_experts(idx_flat, h2_list, seg_counts, gate_all[..., None], w_gate, w_up, w_down,
                      layer)
    outs, slot_base = [], 0
    for n, cap in zip(dims, caps):
        outs.append(_combine(idx_flat, ye, batch=batch, n=n, cap=cap, slot_base=slot_base))
        slot_base += batch * cap
    return outs


def _forward(x, c, ctx, c_ctx, w_mod, b_mod, ln_g, ln_b, na_w_qkv, na_w_o, na_rpb, s5_lam_re,
             s5_lam_im, s5_log_step, s5_b_re, s5_b_im, s5_c_re, s5_c_im, s5_d, s5_w_val, s5_w_gate,
             moe_w_router, moe_w_gate, moe_w_up, moe_w_down, *, n_heads):
    batch, seq, d = x.shape
    ctx_len = ctx.shape[1]
    depth = w_mod.shape[0]
    assert depth == 2
    alpha = (2 * depth) ** 0.25

    cond = jnp.concatenate([c, c_ctx[None, :]], axis=0)
    m = _modulation(cond, w_mod, b_mod)
    x_lat = x.reshape(batch * seq, d)
    x_ctx = ctx.reshape(batch * ctx_len, d)
    lat = dict(rows_per_mod=seq, row0=0)
    cx = dict(rows_per_mod=batch * ctx_len, row0=batch)

    w_qkv = na_w_qkv[0].astype(bf16)
    w_o = na_w_o[0].astype(bf16)
    qkv_lat = _mod_proj(x_lat, m, 0, w_qkv, **lat)
    qkv_ctx = _mod_proj(x_ctx, m, 0, w_qkv, **cx)
    o_lat, o_ctx = _attention(qkv_lat, qkv_ctx, na_rpb[0], batch=batch, seq=seq, ctx_len=ctx_len,
                              n_heads=n_heads)
    wr_t = moe_w_router[0].T
    x1_lat, h2_lat, lg_lat = _mixer_post(o_lat, [w_o], x_lat, m, 0, ln_g[0, 0], ln_b[0, 0], wr_t,
                                         alpha=alpha, **lat)
    x1_ctx, h2_ctx, lg_ctx = _mixer_post(o_ctx, [w_o], x_ctx, m, 0, ln_g[0, 0], ln_b[0, 0], wr_t,
                                         alpha=alpha, **cx)
    f_lat, f_ctx = _moe_layer([h2_lat, h2_ctx], [lg_lat, lg_ctx], [seq, ctx_len], moe_w_gate,
                              moe_w_up, moe_w_down, 0, batch=batch)
    x_lat = _ffn_post(x1_lat, f_lat, m, 0, ln_g[0, 1], ln_b[0, 1], alpha=alpha, **lat)
    x_ctx = _ffn_post(x1_ctx, f_ctx, m, 0, ln_g[0, 1], ln_b[0, 1], alpha=alpha, **cx)

    ar, ai, br, bi = _s5_discretize(s5_lam_re[0], s5_lam_im[0], s5_log_step[0], s5_b_re[0],
                                    s5_b_im[0])
    a_blk, wb_blk, cm_blk = _s5_block_params(ar, ai, br, bi, s5_c_re[0], s5_c_im[0])
    z = _s5_core(x_lat, x_ctx, m, 1, a_blk, wb_blk, cm_blk, s5_d[0], batch=batch, n=seq,
                 nc=ctx_len)
    x1_lat, h2_lat, lg_lat = _mixer_post(z, [s5_w_val[0].astype(bf16), s5_w_gate[0].astype(bf16)],
                                         x_lat, m, 1, ln_g[1, 0], ln_b[1, 0], moe_w_router[1].T,
                                         alpha=alpha, **lat)
    (f_lat,) = _moe_layer([h2_lat], [lg_lat], [seq], moe_w_gate, moe_w_up, moe_w_down, 1,
                          batch=batch)
    x_lat = _ffn_post(x1_lat, f_lat, m, 1, ln_g[1, 1], ln_b[1, 1], alpha=alpha, **lat)
    return x_lat.reshape(batch, seq, d)


def kernel(x, c, ctx, c_ctx, w_mod, b_mod, ln_g, ln_b, na_w_qkv, na_w_o, na_rpb, s5_lam_re,
           s5_lam_im, s5_log_step, s5_b_re, s5_b_im, s5_c_re, s5_c_im, s5_d, s5_w_val, s5_w_gate,
           moe_w_router, moe_w_gate, moe_w_up, moe_w_down):
    return _forward(x, c, ctx, c_ctx, w_mod, b_mod, ln_g, ln_b, na_w_qkv, na_w_o, na_rpb,
                    s5_lam_re, s5_lam_im, s5_log_step, s5_b_re, s5_b_im, s5_c_re, s5_c_im, s5_d,
                    s5_w_val, s5_w_gate, moe_w_router, moe_w_gate, moe_w_up, moe_w_down,
                    n_heads=_N_HEADS)
```

```python
import functools
import math

import jax
import jax.numpy as jnp
from jax import lax
from jax.experimental import pallas as pl
from jax.experimental.pallas import tpu as pltpu

f32 = jnp.float32
bf16 = jnp.bfloat16
i32 = jnp.int32

_GRID_W = 64
_WIN_R = 8
_WIN_C = 16
_N_HEADS = 16
_SSM_GROUP = 16
_CAPACITY_FACTOR = 2
_LN_EPS = 1e-5
_NEG_INF = -1e30

_LANES = 128
_SUBLANES = 8
_VMEM_LIMIT_BYTES = 56 * 1024 * 1024

_NT_DIMS = (((1,), (1,)), ((), ()))


def _cparams(sem):
    return pltpu.CompilerParams(dimension_semantics=sem, vmem_limit_bytes=_VMEM_LIMIT_BYTES)


_MOD_UNROLL = 4


def _mod_body(cb_ref, w_ref, b_ref, o_ref, s_scr, *, n_rows):
    d, tn = w_ref.shape
    reps = tn // _LANES

    @pl.when((pl.program_id(0) == 0) & (pl.program_id(1) == 0))
    def _():
        cv = cb_ref[...]
        s_scr[...] = cv * jax.nn.sigmoid(cv)

    def step(i, accs):
        accs = list(accs)
        for v in range(_MOD_UNROLL):
            k0 = pl.multiple_of((i * _MOD_UNROLL + v) * _SUBLANES, _SUBLANES)
            w = w_ref[pl.ds(k0, _SUBLANES), :]
            for r in range(n_rows):
                s = s_scr[r, pl.ds(k0, _SUBLANES), :]
                accs[r] = accs[r] + jnp.tile(s, (1, reps)) * w
        return tuple(accs)

    init = tuple(jnp.zeros((_SUBLANES, tn), f32) for _ in range(n_rows))
    accs = lax.fori_loop(0, d // (_SUBLANES * _MOD_UNROLL), step, init)
    rows = [jnp.sum(a, axis=0, keepdims=True) + b_ref[...] for a in accs]
    rows.append(jnp.zeros((_SUBLANES - n_rows, tn), f32))
    o_ref[...] = jnp.concatenate(rows, axis=0)


def _modulation(cond, w_mod, b_mod):
    n_rows, d = cond.shape
    depth, _, n6 = w_mod.shape
    tn = min(1024, n6)
    cb = jnp.broadcast_to(cond[:, :, None], (n_rows, d, _LANES))
    return pl.pallas_call(
        functools.partial(_mod_body, n_rows=n_rows),
        grid=(depth, n6 // tn),
        in_specs=[
            pl.BlockSpec((n_rows, d, _LANES), lambda l, j: (0, 0, 0)),
            pl.BlockSpec((None, d, tn), lambda l, j: (l, 0, j)),
            pl.BlockSpec((None, 1, tn), lambda l, j: (l, 0, j)),
        ],
        out_specs=pl.BlockSpec((None, _SUBLANES, tn), lambda l, j: (l, 0, j)),
        out_shape=jax.ShapeDtypeStruct((depth, _SUBLANES, n6), f32),
        scratch_shapes=[pltpu.VMEM((n_rows, d, _LANES), f32)],
        compiler_params=_cparams(("arbitrary", "arbitrary")),
        name="modulation",
    )(cb, w_mod, b_mod.reshape(depth, 1, n6))


def _mod_spec(layer, chunk, d, nargs):
    if nargs == 1:
        return pl.BlockSpec((None, _SUBLANES, d), lambda i: (layer, 0, chunk))
    return pl.BlockSpec((None, _SUBLANES, d), lambda i, j: (layer, 0, chunk))


def _proj_body(x_ref, sc_ref, sh_ref, w_ref, o_ref, h_scr, *, tiles_per_row, row0):
    @pl.when(pl.program_id(1) == 0)
    def _():
        r = row0 + pl.program_id(0) // tiles_per_row
        sc = sc_ref[pl.ds(r, 1), :]
        sh = sh_ref[pl.ds(r, 1), :]
        h_scr[...] = (x_ref[...] * (1.0 + sc) + sh).astype(bf16)

    o_ref[...] = jnp.dot(h_scr[...], w_ref[...], preferred_element_type=f32).astype(o_ref.dtype)


def _mod_proj(x, m, layer, w_bf, *, rows_per_mod, row0):
    mtot, d = x.shape
    n = w_bf.shape[1]
    tm = min(1024, rows_per_mod, mtot)
    tn = min(512, n)
    return pl.pallas_call(
        functools.partial(_proj_body, tiles_per_row=rows_per_mod // tm, row0=row0),
        grid=(mtot // tm, n // tn),
        in_specs=[
            pl.BlockSpec((tm, d), lambda i, j: (i, 0)),
            _mod_spec(layer, 1, d, 2),
            _mod_spec(layer, 0, d, 2),
            pl.BlockSpec((d, tn), lambda i, j: (0, j)),
        ],
        out_specs=pl.BlockSpec((tm, tn), lambda i, j: (i, j)),
        out_shape=jax.ShapeDtypeStruct((mtot, n), bf16),
        scratch_shapes=[pltpu.VMEM((tm, d), bf16)],
        compiler_params=_cparams(("arbitrary", "arbitrary")),
        name="mod_proj",
    )(x, m, m, w_bf)


_ATTN_PAIR_UNROLL = 4
_ATTN_VT_UNROLL = 4
_ATTN_KEY_ROWS = _WIN_R + 2


def _attn_body(rpb_ref, q_ref, k_ref, v_ref, qc_ref, kc_ref, vc_ref, o_ref, oc_ref, t_scr, vt_scr,
               *, rows, dh):
    w = _GRID_W
    n_dr = 2 * _WIN_R - 1
    n_dc = 2 * _WIN_C - 1
    masked = n_dr
    h = pl.program_id(0)
    scale = dh ** -0.5
    lane = lax.broadcasted_iota(i32, (w, 2 * w), 1)
    first_row = lane < w

    @pl.when(pl.program_id(1) == 0)
    def _build():
        kc = lax.broadcasted_iota(i32, (w, 2 * w), 0)
        c = lane & (w - 1)
        delta = jnp.clip(kc - c + (_WIN_C - 1), 0, n_dc - 1)
        cstart = jnp.clip(c - _WIN_C // 2, 0, w - _WIN_C)
        valid = (kc >= cstart) & (kc < cstart + _WIN_C)
        base = h * (n_dr * n_dc)

        def dr_body(dr, carry):
            acc = jnp.zeros((w, 2 * w), f32)
            for d in range(n_dc):
                acc = jnp.where(delta == d, rpb_ref[base + dr * n_dc + d], acc)
            t_scr[dr] = jnp.where(valid, acc, _NEG_INF)
            return carry

        lax.fori_loop(0, n_dr, dr_body, 0)
        t_scr[masked] = jnp.full((w, 2 * w), _NEG_INF, f32)

    def vt_body(i, carry):
        for u in range(_ATTN_VT_UNROLL):
            blk_i = i * _ATTN_VT_UNROLL + u
            blk = v_ref[pl.ds(pl.multiple_of(blk_i * 2 * w, 2 * w), 2 * w), :]
            vt_scr[blk_i] = blk.astype(f32).T.astype(bf16)
        return carry

    lax.fori_loop(0, rows // (2 * _ATTN_VT_UNROLL), vt_body, 0)
    kcx = kc_ref[...]
    vcx = vc_ref[...]
    vct = vcx.astype(f32).T.astype(bf16)

    def pair_scores(i):
        r = 2 * i
        band = jnp.minimum(jnp.clip(r - _WIN_R // 2, 0, rows - _WIN_R - 1) & -2,
                           rows - _ATTN_KEY_ROWS)
        q0 = pl.multiple_of(r * w, 2 * w)
        k0 = pl.multiple_of(band * w, 2 * w)
        q2 = q_ref[pl.ds(q0, 2 * w), :]
        kw = k_ref[pl.ds(k0, _ATTN_KEY_ROWS * w), :]
        blocks = []
        drs = []
        for u in range(2):
            rs = jnp.clip(r + u - _WIN_R // 2, 0, rows - _WIN_R)
            drs.append((rs - (r + u) + (_WIN_R - 1), rs - band))
        for kr in range(_ATTN_KEY_ROWS):
            idx = []
            for s0, off in drs:
                rel = kr - off
                idx.append(jnp.where((rel >= 0) & (rel < _WIN_R), s0 + rel, masked))
            blocks.append(jnp.where(first_row, t_scr[idx[0]], t_scr[idx[1]]))
        bias = jnp.concatenate(blocks, axis=0)
        s_loc = lax.dot_general(kw, q2, _NT_DIMS, preferred_element_type=f32) * scale + bias
        s_ctx = lax.dot_general(kcx, q2, _NT_DIMS, preferred_element_type=f32) * scale
        return s_loc, s_ctx, band, q0

    def pair_output(s_loc, s_ctx, band, q0):
        mx = jnp.maximum(jnp.max(s_loc, axis=0, keepdims=True),
                         jnp.max(s_ctx, axis=0, keepdims=True))
        p_loc = jnp.exp(s_loc - mx)
        p_ctx = jnp.exp(s_ctx - mx)
        den = jnp.sum(p_loc, axis=0, keepdims=True) + jnp.sum(p_ctx, axis=0, keepdims=True)
        vb = band // 2
        vwt = jnp.concatenate([vt_scr[vb + t] for t in range(_ATTN_KEY_ROWS // 2)], axis=1)
        o_t = (jnp.dot(vwt, p_loc.astype(bf16), preferred_element_type=f32)
               + jnp.dot(vct, p_ctx.astype(bf16), preferred_element_type=f32))
        o_ref[pl.ds(q0, 2 * w), :] = (o_t / den).T.astype(o_ref.dtype)

    def pair_body(i, carry):
        scores = [pair_scores(i * _ATTN_PAIR_UNROLL + u) for u in range(_ATTN_PAIR_UNROLL)]
        for sc in scores:
            pair_output(*sc)
        return carry

    lax.fori_loop(0, rows // (2 * _ATTN_PAIR_UNROLL), pair_body, 0)

    s = lax.dot_general(qc_ref[...], kcx, _NT_DIMS, preferred_element_type=f32) * scale
    p = jnp.exp(s - jnp.max(s, axis=1, keepdims=True))
    den = jnp.sum(p, axis=1, keepdims=True)
    oc = jnp.dot(p.astype(bf16), vcx, preferred_element_type=f32)
    oc_ref[...] = (oc / den).astype(oc_ref.dtype)


def _attention(qkv_lat, qkv_ctx, rpb, *, batch, seq, ctx_len, n_heads):
    d = qkv_lat.shape[1] // 3
    dh = d // n_heads
    rows = seq // _GRID_W
    assert dh == _LANES and _GRID_W * 2 == _LANES and rows >= _WIN_R
    assert rows % (2 * _ATTN_PAIR_UNROLL) == 0 and rows >= _ATTN_KEY_ROWS
    assert rows % (2 * _ATTN_VT_UNROLL) == 0
    n_dr = 2 * _WIN_R - 1
    return pl.pallas_call(
        functools.partial(_attn_body, rows=rows, dh=dh),
        grid=(n_heads, batch),
        in_specs=[
            pl.BlockSpec(memory_space=pltpu.SMEM),
            pl.BlockSpec((seq, dh), lambda h, b: (b, h)),
            pl.BlockSpec((seq, dh), lambda h, b: (b, n_heads + h)),
            pl.BlockSpec((seq, dh), lambda h, b: (b, 2 * n_heads + h)),
            pl.BlockSpec((ctx_len, dh), lambda h, b: (b, h)),
            pl.BlockSpec((ctx_len, dh), lambda h, b: (b, n_heads + h)),
            pl.BlockSpec((ctx_len, dh), lambda h, b: (b, 2 * n_heads + h)),
        ],
        out_specs=[
            pl.BlockSpec((seq, dh), lambda h, b: (b, h)),
            pl.BlockSpec((ctx_len, dh), lambda h, b: (b, h)),
        ],
        out_shape=[
            jax.ShapeDtypeStruct((batch * seq, d), bf16),
            jax.ShapeDtypeStruct((batch * ctx_len, d), bf16),
        ],
        scratch_shapes=[
            pltpu.VMEM((n_dr + 1, _GRID_W, 2 * _GRID_W), f32),
            pltpu.VMEM((rows // 2, dh, 2 * _GRID_W), bf16),
        ],
        compiler_params=_cparams(("arbitrary", "arbitrary")),
        name="nbr_attention",
    )(rpb.reshape(-1), qkv_lat, qkv_lat, qkv_lat, qkv_ctx, qkv_ctx, qkv_ctx)


def _layer_norm(y, g, b):
    mu = jnp.mean(y, axis=-1, keepdims=True)
    yc = y - mu
    var = jnp.mean(yc * yc, axis=-1, keepdims=True)
    return yc * lax.rsqrt(var + _LN_EPS) * g + b


def _post_body(*refs, glu, alpha, tiles_per_row, row0):
    if glu:
        a_ref, wv_ref, wg_ref = refs[:3]
        rest = refs[3:]
    else:
        a_ref, wv_ref = refs[:2]
        rest = refs[2:]
    x_ref, g1_ref, sc2_ref, sh2_ref, lng_ref, lnb_ref, wr_ref, x1_ref, h2_ref, lg_ref = rest
    a = a_ref[...]
    o = jnp.dot(a, wv_ref[...], preferred_element_type=f32)
    if glu:
        o = o * jax.nn.sigmoid(jnp.dot(a, wg_ref[...], preferred_element_type=f32))
    r = row0 + pl.program_id(0) // tiles_per_row
    g1 = g1_ref[pl.ds(r, 1), :]
    x1 = _layer_norm(alpha * x_ref[...] + g1 * o, lng_ref[...], lnb_ref[...])
    h2 = x1 * (1.0 + sc2_ref[pl.ds(r, 1), :]) + sh2_ref[pl.ds(r, 1), :]
    x1_ref[...] = x1
    _store_token_tiles(h2_ref, _pack_bf16_pairs(h2))
    n_exp = wr_ref.shape[0]
    wr = wr_ref[...]
    w_hi = wr.astype(bf16)
    w_lo = (wr - w_hi.astype(f32)).astype(bf16)
    h_hi = h2.astype(bf16)
    h_lo = (h2 - h_hi.astype(f32)).astype(bf16)
    both = lax.dot_general(jnp.concatenate([w_hi, w_lo], axis=0), h_hi, _NT_DIMS,
                           preferred_element_type=f32)
    lg_ref[...] = (both[:n_exp] + both[n_exp:]
                   + lax.dot_general(w_hi, h_lo, _NT_DIMS, preferred_element_type=f32))


def _mixer_post(a_bf, weights_bf, x, m, layer, ln_g, ln_b, w_router_t, *, alpha, rows_per_mod,
                row0):
    mtot, d = x.shape
    n_exp = w_router_t.shape[0]
    glu = len(weights_bf) == 2
    tm = min(256 if glu else 512, mtot, rows_per_mod)
    tile_rows = d // 2 // _LANES
    const = lambda i: (0, 0)
    in_specs = [pl.BlockSpec((tm, d), lambda i: (i, 0))]
    in_specs += [pl.BlockSpec((d, d), const, pipeline_mode=pl.Buffered(1)) for _ in weights_bf]
    in_specs += [
        pl.BlockSpec((tm, d), lambda i: (i, 0)),
        _mod_spec(layer, 2, d, 1),
        _mod_spec(layer, 4, d, 1),
        _mod_spec(layer, 3, d, 1),
        pl.BlockSpec((1, d), const),
        pl.BlockSpec((1, d), const),
        pl.BlockSpec((n_exp, d), const),
    ]
    return pl.pallas_call(
        functools.partial(_post_body, glu=glu, alpha=alpha, tiles_per_row=rows_per_mod // tm,
                          row0=row0),
        grid=(mtot // tm,),
        in_specs=in_specs,
        out_specs=[
            pl.BlockSpec((tm, d), lambda i: (i, 0)),
            pl.BlockSpec((tm * tile_rows, _LANES), lambda i: (i, 0)),
            pl.BlockSpec((n_exp, tm), lambda i: (0, i)),
        ],
        out_shape=[
            jax.ShapeDtypeStruct((mtot, d), f32),
            jax.ShapeDtypeStruct((mtot * tile_rows, _LANES), jnp.uint32),
            jax.ShapeDtypeStruct((n_exp, mtot), f32),
        ],
        compiler_params=_cparams(("arbitrary",)),
        name="mixer_post_glu" if glu else "mixer_post",
    )(a_bf, *weights_bf, x, m, m, m, ln_g.reshape(1, d), ln_b.reshape(1, d), w_router_t)


def _cumsum_excl(mask01):
    rows, n = mask01.shape
    r = lax.broadcasted_iota(i32, (_LANES, _LANES), 0)
    c = lax.broadcasted_iota(i32, (_LANES, _LANES), 1)
    tri = jnp.where(r < c, 1.0, 0.0).astype(bf16)
    carry = jnp.zeros((rows, 1), f32)
    outs = []
    for t in range(n // _LANES):
        blk = mask01[:, t * _LANES:(t + 1) * _LANES]
        outs.append(jnp.dot(blk.astype(bf16), tri, preferred_element_type=f32) + carry)
        carry = carry + jnp.sum(blk, axis=1, keepdims=True)
    return outs[0] if len(outs) == 1 else jnp.concatenate(outs, axis=1)


def _route_body(lg_ref, idx_ref, gate_ref, posm_scr, aff_scr, *, n, cap, chunk):
    n_exp = lg_ref.shape[0]
    x = lg_ref[...]
    ex = jnp.exp(x - jnp.max(x, axis=0, keepdims=True))
    aff = ex / jnp.sum(ex, axis=0, keepdims=True)
    bits = pltpu.bitcast(aff, i32)

    thr = jnp.zeros((n_exp, 1), i32)
    for bit in range(30, -1, -1):
        cand = thr | (1 << bit)
        cnt = jnp.sum(jnp.where(bits >= cand, 1.0, 0.0), axis=1, keepdims=True)
        thr = jnp.where(cnt >= cap, cand, thr)
    gt = bits > thr
    eq = bits == thr
    need = cap - jnp.sum(jnp.where(gt, 1.0, 0.0), axis=1, keepdims=True)
    rank_eq = _cumsum_excl(jnp.where(eq, 1.0, 0.0))
    sel = gt | (eq & (rank_eq < need))
    pos = _cumsum_excl(jnp.where(sel, 1.0, 0.0))
    posm_scr[...] = jnp.where(sel, pos, -1.0)
    a_hi = aff.astype(bf16)
    r1 = aff - a_hi.astype(f32)
    a_mid = r1.astype(bf16)
    a_lo = (r1 - a_mid.astype(f32)).astype(bf16)
    for k, piece in enumerate((a_hi, a_mid, a_lo)):
        aff_scr[k] = piece.astype(f32)
    tok = lax.broadcasted_iota(i32, (1, n), 1)
    digits = jnp.concatenate([(tok >> 6).astype(f32), (tok & 63).astype(f32)], axis=0)
    pad = jnp.zeros((_SUBLANES - 5, n), f32)
    tok_base = pl.program_id(0) * n

    def per_expert(e, carry):
        prow = posm_scr[pl.ds(e, 1), :]
        rows = [digits] + [aff_scr[k, pl.ds(e, 1), :] for k in range(3)] + [pad]
        vals = jnp.concatenate(rows, axis=0).astype(bf16)
        outs = []
        for pc in range(cap // chunk):
            slot = (lax.broadcasted_iota(i32, (chunk, n), 0) + pc * chunk).astype(f32)
            hit = jnp.where(prow == slot, 1.0, 0.0).astype(bf16)
            outs.append(lax.dot_general(vals, hit, _NT_DIMS, preferred_element_type=f32))
        got = outs[0] if len(outs) == 1 else jnp.concatenate(outs, axis=1)
        idx = got[0:1] * 64.0 + got[1:2]
        idx_ref[pl.ds(e, 1), :] = idx.astype(i32) + tok_base
        gate_ref[pl.ds(e, 1), :] = got[2:3] + got[3:4] + got[4:5]
        return carry

    lax.fori_loop(0, n_exp, per_expert, 0)


def _route(logits_t, *, batch, n):
    n_exp = logits_t.shape[0]
    cap = _CAPACITY_FACTOR * n // n_exp
    chunk = min(_LANES, cap)
    assert n % _LANES == 0 and cap % chunk == 0 and n <= 64 * 64
    return pl.pallas_call(
        functools.partial(_route_body, n=n, cap=cap, chunk=chunk),
        grid=(batch,),
        in_specs=[pl.BlockSpec((n_exp, n), lambda b: (0, b))],
        out_specs=[
            pl.BlockSpec((None, n_exp, cap), lambda b: (b, 0, 0)),
            pl.BlockSpec((None, n_exp, cap), lambda b: (b, 0, 0)),
        ],
        out_shape=[
            jax.ShapeDtypeStruct((batch, n_exp, cap), i32),
            jax.ShapeDtypeStruct((batch, n_exp, cap), f32),
        ],
        scratch_shapes=[pltpu.VMEM((n_exp, n), f32), pltpu.VMEM((3, n_exp, n), f32)],
        compiler_params=_cparams(("arbitrary",)),
        name="route_topk",
    )(logits_t)


def _pack_bf16_pairs(h):
    half = h.shape[1] // 2
    u = pltpu.bitcast(h.astype(bf16).astype(f32), jnp.uint32)
    return (u[:, :half] & jnp.uint32(0xFFFF0000)) | (u[:, half:] >> 16)


def _store_token_tiles(ref, packed):
    m, width = packed.shape
    r = width // _LANES
    for s in range(r):
        ref[pl.ds(s, m, stride=r), :] = packed[:, s * _LANES:(s + 1) * _LANES]


def _load_token_tiles(ref, m):
    r = ref.shape[0] // m
    chunks = [ref[pl.ds(s, m, stride=r), :] for s in range(r)]
    return jnp.concatenate(chunks, axis=1)


def _unpack_bf16_pairs(p):
    hi = pltpu.bitcast(p & jnp.uint32(0xFFFF0000), f32).astype(bf16)
    lo = pltpu.bitcast(p << 16, f32).astype(bf16)
    return jnp.concatenate([hi, lo], axis=1)


_GATHER_UNROLL = 8


def _moe_body(idx_ref, *refs, seg_counts, m_slots, n_tiles):
    n_src = len(seg_counts)
    srcs = refs[:n_src]
    wg_ref, wu_ref, wd_ref, o_ref, xp_scr, xe16, hmid, sem = refs[n_src:]
    e = pl.program_id(0)
    j = pl.program_id(1)
    tf = wg_ref.shape[1]

    tile_rows = xp_scr.shape[0] // m_slots

    def tile_copy(src, row, slot):
        r0 = pl.multiple_of(row * tile_rows, tile_rows)
        s0 = pl.multiple_of(slot * tile_rows, tile_rows)
        return pltpu.make_async_copy(src.at[pl.ds(r0, tile_rows), :],
                                     xp_scr.at[pl.ds(s0, tile_rows), :], sem)

    def for_each_slot(fn):
        slot0 = 0
        for src, cnt in zip(srcs, seg_counts):
            def body(i, c, src=src, slot0=slot0):
                for u in range(_GATHER_UNROLL):
                    fn(src, slot0 + i * _GATHER_UNROLL + u)
                return c

            lax.fori_loop(0, cnt // _GATHER_UNROLL, body, 0)
            slot0 += cnt

    def issue_gather(expert):
        base = expert * m_slots
        for_each_slot(lambda src, slot: tile_copy(src, idx_ref[base + slot], slot).start())

    @pl.when(j == 0)
    def _rows():
        @pl.when(e == 0)
        def _():
            issue_gather(0)

        for_each_slot(lambda src, slot: tile_copy(src, 0, slot).wait())
        xe16[...] = _unpack_bf16_pairs(_load_token_tiles(xp_scr, m_slots))

        @pl.when(e + 1 < pl.num_programs(0))
        def _():
            issue_gather(e + 1)

    @pl.when(j < n_tiles)
    def _up():
        x = xe16[...]
        a = jnp.dot(x, wg_ref[...].astype(bf16), preferred_element_type=f32)
        u = jnp.dot(x, wu_ref[...].astype(bf16), preferred_element_type=f32)
        hmid[j] = (a * jax.nn.sigmoid(a) * u).astype(bf16)

    @pl.when(j >= n_tiles)
    def _down():
        y = jnp.dot(hmid[0], wd_ref[0:tf, :].astype(bf16), preferred_element_type=f32)
        for k in range(1, n_tiles):
            y += jnp.dot(hmid[k], wd_ref[k * tf:(k + 1) * tf, :].astype(bf16),
                         preferred_element_type=f32)
        o_ref[...] = y


def _moe_experts(idx_flat, srcs, seg_counts, w_gate, w_up, w_down, layer):
    _, n_exp, d, ff = w_gate.shape
    m_slots = sum(seg_counts)
    tf = min(512, ff)
    n_tiles = ff // tf
    assert d // tf == n_tiles and all(c % _GATHER_UNROLL == 0 for c in seg_counts)
    n_src = len(srcs)
    up_idx = lambda e, j, idx: (layer, e, 0, jnp.minimum(j, n_tiles - 1))
    dn_idx = lambda e, j, idx: (layer, e, 0, jnp.maximum(j - n_tiles, 0))
    grid_spec = pltpu.PrefetchScalarGridSpec(
        num_scalar_prefetch=1,
        grid=(n_exp, 2 * n_tiles),
        in_specs=[pl.BlockSpec(memory_space=pl.ANY)] * n_src + [
            pl.BlockSpec((None, None, d, tf), up_idx),
            pl.BlockSpec((None, None, d, tf), up_idx),
            pl.BlockSpec((None, None, ff, tf), dn_idx),
        ],
        out_specs=pl.BlockSpec((None, m_slots, tf),
                               lambda e, j, idx: (e, 0, jnp.maximum(j - n_tiles, 0))),
        scratch_shapes=[
            pltpu.VMEM((m_slots * (d // 2 // _LANES), _LANES), jnp.uint32),
            pltpu.VMEM((m_slots, d), bf16),
            pltpu.VMEM((n_tiles, m_slots, tf), bf16),
            pltpu.SemaphoreType.DMA(()),
        ],
    )
    return pl.pallas_call(
        functools.partial(_moe_body, seg_counts=tuple(seg_counts), m_slots=m_slots,
                          n_tiles=n_tiles),
        grid_spec=grid_spec,
        out_shape=jax.ShapeDtypeStruct((n_exp, m_slots, d), f32),
        compiler_params=_cparams(("arbitrary", "arbitrary")),
        name="moe_experts",
    )(idx_flat, *srcs, w_gate, w_up, w_down)


_SCATTER_UNROLL = 2


def _combine_body(idx_ref, gate_ref, ye_ref, o_hbm, acc, sem, *, m_slots, slot_base, cap, n):
    b = pl.program_id(0)
    e = pl.program_id(1)

    @pl.when(e == 0)
    def _():
        acc[...] = jnp.zeros_like(acc)

    base = e * m_slots + slot_base + b * cap
    tok0 = b * n

    def body(i, carry):
        j0 = i * _SCATTER_UNROLL
        toks = [idx_ref[base + j0 + u] - tok0 for u in range(_SCATTER_UNROLL)]
        vals = [acc[pl.ds(toks[u], 1), :] + gate_ref[base + j0 + u] * ye_ref[pl.ds(j0 + u, 1), :]
                for u in range(_SCATTER_UNROLL)]
        for u in range(_SCATTER_UNROLL):
            acc[pl.ds(toks[u], 1), :] = vals[u]
        return carry

    lax.fori_loop(0, cap // _SCATTER_UNROLL, body, 0)

    @pl.when(e == pl.num_programs(1) - 1)
    def _():
        out = pltpu.make_async_copy(acc, o_hbm.at[pl.ds(pl.multiple_of(b * n, n), n), :], sem)
        out.start()
        out.wait()


def _combine(idx_flat, gate_flat, ye, *, batch, n, cap, slot_base):
    n_exp, m_slots, d = ye.shape
    assert slot_base % cap == 0 and cap % _SCATTER_UNROLL == 0
    blk0 = slot_base // cap
    grid_spec = pltpu.PrefetchScalarGridSpec(
        num_scalar_prefetch=2,
        grid=(batch, n_exp),
        in_specs=[pl.BlockSpec((None, cap, d), lambda b, e, idx, gate: (e, blk0 + b, 0))],
        out_specs=pl.BlockSpec(memory_space=pl.ANY),
        scratch_shapes=[pltpu.VMEM((n, d), f32), pltpu.SemaphoreType.DMA(())],
    )
    return pl.pallas_call(
        functools.partial(_combine_body, m_slots=m_slots, slot_base=slot_base, cap=cap, n=n),
        grid_spec=grid_spec,
        out_shape=jax.ShapeDtypeStruct((batch * n, d), f32),
        compiler_params=_cparams(("arbitrary", "arbitrary")),
        name="moe_combine",
    )(idx_flat, gate_flat, ye)


def _ffn_post_body(x_ref, f_ref, g2_ref, lng_ref, lnb_ref, o_ref, *, alpha, tiles_per_row, row0):
    r = row0 + pl.program_id(0) // tiles_per_row
    g2 = g2_ref[pl.ds(r, 1), :]
    o_ref[...] = _layer_norm(alpha * x_ref[...] + g2 * f_ref[...], lng_ref[...], lnb_ref[...])


def _ffn_post(x1, fo, m, layer, ln_g, ln_b, *, alpha, rows_per_mod, row0):
    mtot, d = x1.shape
    tm = min(512, mtot, rows_per_mod)
    const = lambda i: (0, 0)
    return pl.pallas_call(
        functools.partial(_ffn_post_body, alpha=alpha, tiles_per_row=rows_per_mod // tm, row0=row0),
        grid=(mtot // tm,),
        in_specs=[
            pl.BlockSpec((tm, d), lambda i: (i, 0)),
            pl.BlockSpec((tm, d), lambda i: (i, 0)),
            _mod_spec(layer, 5, d, 1),
            pl.BlockSpec((1, d), const),
            pl.BlockSpec((1, d), const),
        ],
        out_specs=pl.BlockSpec((tm, d), lambda i: (i, 0)),
        out_shape=jax.ShapeDtypeStruct((mtot, d), f32),
        compiler_params=_cparams(("arbitrary",)),
        name="ffn_post",
    )(x1, fo, m, ln_g.reshape(1, d), ln_b.reshape(1, d))


def _disc_body(lr_ref, li_ref, ls_ref, bre_ref, bim_ref, ar_ref, ai_ref, br_ref, bi_ref):
    lr = jnp.minimum(lr_ref[...], -1e-4)
    li = li_ref[...]
    dt = jnp.exp(ls_ref[...])
    mag = jnp.exp(lr * dt)
    ar = mag * jnp.cos(li * dt)
    ai = mag * jnp.sin(li * dt)
    nr = ar - 1.0
    den = lr * lr + li * li
    cr = (nr * lr + ai * li) / den
    ci = (ai * lr - nr * li) / den
    ar_ref[...] = ar
    ai_ref[...] = ai
    br_ref[...] = cr * bre_ref[...] - ci * bim_ref[...]
    bi_ref[...] = cr * bim_ref[...] + ci * bre_ref[...]


def _s5_discretize(lam_re, lam_im, log_step, b_re, b_im):
    shape = b_re.shape
    flat = (shape[0] * shape[1], shape[2] * shape[3])
    bc = lambda a: jnp.broadcast_to(a, shape).reshape(flat)
    outs = pl.pallas_call(
        _disc_body,
        out_shape=[jax.ShapeDtypeStruct(flat, f32)] * 4,
        name="s5_discretize",
    )(bc(lam_re[..., None]), bc(lam_im[..., None]), bc(log_step[:, :, None, None]),
      b_re.reshape(flat), b_im.reshape(flat))
    ar, ai, br, bi = [o.reshape(shape) for o in outs]
    return ar[..., 0], ai[..., 0], br, bi


_SCAN_UNROLL = 8


def _cmul(ar, ai, xr, xi):
    return ar * xr - ai * xi, ar * xi + ai * xr


def _cpow(ar, ai, k):
    rr, ri = jnp.ones_like(ar), jnp.zeros_like(ai)
    while k:
        if k & 1:
            rr, ri = _cmul(ar, ai, rr, ri)
        ar, ai = _cmul(ar, ai, ar, ai)
        k >>= 1
    return rr, ri


def _segment_scan(bu_ref, seg_len, s_dim, ar, ai, init, reverse, xs_ref=None):
    ns = _SUBLANES
    unroll = math.gcd(seg_len, _SCAN_UNROLL)
    assert xs_ref is None or unroll % 2 == 0

    def outer(jo, carry):
        xr, xi = carry
        prev = None
        for ji in range(unroll):
            jj = jo * unroll + ji
            j = (seg_len - 1 - jj) if reverse else jj
            r0 = pl.multiple_of(j * ns, ns)
            br = bu_ref[pl.ds(r0, ns), 0:s_dim]
            bi = bu_ref[pl.ds(r0, ns), s_dim:2 * s_dim]
            xr, xi = ar * xr - ai * xi + br, ar * xi + ai * xr + bi
            if xs_ref is not None:
                if ji % 2 == 0:
                    prev = (xr, xi)
                else:
                    pair = ((xr, prev[0]), (xi, prev[1])) if reverse else ((prev[0], xr),
                                                                          (prev[1], xi))
                    p0 = pl.multiple_of((j if reverse else j - 1) * ns, 2 * ns)
                    for half, (lo, hi) in enumerate(pair):
                        xs_ref[pl.ds(p0, 2 * ns), half * s_dim:(half + 1) * s_dim] = (
                            jnp.concatenate([lo, hi], axis=0).astype(bf16))
        return xr, xi

    return lax.fori_loop(0, seg_len // unroll, outer, init)


def _segment_inits(ends, a_len, h0, reverse):
    er, ei = ends
    alr, ali = a_len
    ns = _SUBLANES
    order = range(ns - 1, -1, -1) if reverse else range(ns)
    cr, ci = h0
    inits_r, inits_i = [None] * ns, [None] * ns
    for s in order:
        inits_r[s], inits_i[s] = cr, ci
        pr, pi = _cmul(alr, ali, cr, ci)
        cr, ci = pr + er[s:s + 1, :], pi + ei[s:s + 1, :]
    return (jnp.concatenate(inits_r, axis=0), jnp.concatenate(inits_i, axis=0)), (cr, ci)


def _s5_body(x_ref, xc_ref, sh_ref, sc_ref, a_ref, wb_ref, cm_ref, d_ref, z_ref,
             up_scr, ucp_scr, bu_scr, xs_scr, buc_scr, y_scr, yn_scr, *, n, nc, batch):
    ns = _SUBLANES
    seg, segc = n // ns, nc // ns
    s_dim = a_ref.shape[-1]
    b = pl.program_id(0)
    sc_l, sh_l = sc_ref[pl.ds(b, 1), :], sh_ref[pl.ds(b, 1), :]
    sc_c, sh_c = sc_ref[batch:batch + 1, :], sh_ref[batch:batch + 1, :]

    for j in range(seg):
        up_scr[j * ns:(j + 1) * ns, :] = x_ref[pl.ds(j, ns, stride=seg), :] * (1.0 + sc_l) + sh_l
    for j in range(segc):
        ucp_scr[j * ns:(j + 1) * ns, :] = xc_ref[pl.ds(j, ns, stride=segc), :] * (1.0 + sc_c) + sh_c

    u = up_scr[...]
    y_scr[...] = d_ref[...] * u
    uc_bf = ucp_scr[...].astype(bf16)
    zero = (jnp.zeros((ns, s_dim), f32), jnp.zeros((ns, s_dim), f32))
    u_bf = u.astype(bf16)
    for direction in range(2):
        reverse = direction == 1
        ar1, ai1 = a_ref[direction, 0:1, :], a_ref[direction, 1:2, :]
        ar = jnp.broadcast_to(ar1, (ns, s_dim))
        ai = jnp.broadcast_to(ai1, (ns, s_dim))
        buc_scr[...] = jnp.dot(uc_bf, wb_ref[direction], preferred_element_type=f32)
        ends_c = _segment_scan(buc_scr, segc, s_dim, ar, ai, zero, reverse)
        zero1 = (jnp.zeros((1, s_dim), f32), jnp.zeros((1, s_dim), f32))
        _, h0 = _segment_inits(ends_c, _cpow(ar1, ai1, segc), zero1, reverse)
        for lo in (0, n // 2):
            bu_scr[lo:lo + n // 2, :] = jnp.dot(u_bf[lo:lo + n // 2], wb_ref[direction],
                                                preferred_element_type=f32)
        ends = _segment_scan(bu_scr, seg, s_dim, ar, ai, zero, reverse)
        inits, _ = _segment_inits(ends, _cpow(ar1, ai1, seg), h0, reverse)
        _segment_scan(bu_scr, seg, s_dim, ar, ai, inits, reverse, xs_ref=xs_scr)
        for lo in (0, n // 2):
            y_scr[lo:lo + n // 2, :] += jnp.dot(xs_scr[lo:lo + n // 2, :], cm_ref[direction],
                                                preferred_element_type=f32)

    for j in range(seg):
        yn_scr[pl.ds(j, ns, stride=seg), :] = y_scr[j * ns:(j + 1) * ns, :]
    z_ref[...] = jax.nn.gelu(yn_scr[...]).astype(z_ref.dtype)


def _s5_core(x_lat, x_ctx, m, layer, a_blk, wb_blk, cm_blk, d_skip, *, batch, n, nc):
    d = x_lat.shape[1]
    n_blk = d // _LANES
    s_dim = a_blk.shape[-1]
    assert n % (_SUBLANES * _SCAN_UNROLL) == 0 and nc % _SUBLANES == 0
    return pl.pallas_call(
        functools.partial(_s5_body, n=n, nc=nc, batch=batch),
        grid=(batch, n_blk),
        in_specs=[
            pl.BlockSpec((n, _LANES), lambda b, g: (b, g)),
            pl.BlockSpec((nc, _LANES), lambda b, g: (b, g)),
            pl.BlockSpec((None, _SUBLANES, _LANES), lambda b, g: (layer, 0, g)),
            pl.BlockSpec((None, _SUBLANES, _LANES), lambda b, g: (layer, 0, n_blk + g)),
            pl.BlockSpec((2, None, 2, s_dim), lambda b, g: (0, g, 0, 0)),
            pl.BlockSpec((2, None, _LANES, 2 * s_dim), lambda b, g: (0, g, 0, 0)),
            pl.BlockSpec((2, None, 2 * s_dim, _LANES), lambda b, g: (0, g, 0, 0)),
            pl.BlockSpec((1, _LANES), lambda b, g: (0, g)),
        ],
        out_specs=pl.BlockSpec((n, _LANES), lambda b, g: (b, g)),
        out_shape=jax.ShapeDtypeStruct((batch * n, d), bf16),
        scratch_shapes=[
            pltpu.VMEM((n, _LANES), f32),
            pltpu.VMEM((nc, _LANES), f32),
            pltpu.VMEM((n, 2 * s_dim), f32),
            pltpu.VMEM((n, 2 * s_dim), bf16),
            pltpu.VMEM((nc, 2 * s_dim), f32),
            pltpu.VMEM((n, _LANES), f32),
            pltpu.VMEM((n, _LANES), f32),
        ],
        compiler_params=_cparams(("arbitrary", "arbitrary")),
        name="s5_core",
    )(x_lat, x_ctx, m, m, a_blk, wb_blk, cm_blk, d_skip.reshape(1, d))


def _s5_block_params(ar, ai, br, bi, c_re, c_im):
    _, n_groups, p_dim, gh = br.shape
    gpb = _LANES // gh
    n_blk = n_groups // gpb
    s_dim = gpb * p_dim
    eye = jnp.eye(gpb, dtype=f32)
    a_blk = jnp.stack([ar.reshape(2, n_blk, s_dim), ai.reshape(2, n_blk, s_dim)], axis=2)

    def b_mat(bx):
        t = bx.reshape(2, n_blk, gpb, p_dim, gh)
        return jnp.einsum("dngph,gk->dnghkp", t, eye).reshape(2, n_blk, gpb * gh, s_dim)

    def c_mat(cx):
        t = cx.reshape(2, n_blk, gpb, gh, p_dim)
        return jnp.einsum("dnghp,gk->dngpkh", t, eye).reshape(2, n_blk, s_dim, gpb * gh)

    wb = jnp.concatenate([b_mat(br), b_mat(bi)], axis=-1).astype(bf16)
    cm = jnp.concatenate([c_mat(c_re), -c_mat(c_im)], axis=-2).astype(bf16)
    return a_blk, wb, cm


def _moe_layer(h2_list, logits_list, dims, w_gate, w_up, w_down, layer, *, batch):
    idxs, gates, caps = [], [], []
    for lg, n in zip(logits_list, dims):
        idx, gate = _route(lg, batch=batch, n=n)
        idxs.append(idx)
        gates.append(gate)
        caps.append(idx.shape[-1])
    n_exp = w_gate.shape[1]
    idx_all = jnp.concatenate([jnp.swapaxes(i, 0, 1).reshape(n_exp, -1) for i in idxs], axis=1)
    gate_all = jnp.concatenate([jnp.swapaxes(g, 0, 1).reshape(n_exp, -1) for g in gates], axis=1)
    seg_counts = [batch * c for c in caps]
    idx_flat = idx_all.reshape(-1)
    gate_flat = gate_all.reshape(-1)
    ye = _moe_experts(idx_flat, h2_list, seg_counts, w_gate, w_up, w_down, layer)
    outs, slot_base = [], 0
    for n, cap in zip(dims, caps):
        outs.append(_combine(idx_flat, gate_flat, ye, batch=batch, n=n, cap=cap,
                             slot_base=slot_base))
        slot_base += batch * cap
    return outs


def _forward(x, c, ctx, c_ctx, w_mod, b_mod, ln_g, ln_b, na_w_qkv, na_w_o, na_rpb, s5_lam_re,
             s5_lam_im, s5_log_step, s5_b_re, s5_b_im, s5_c_re, s5_c_im, s5_d, s5_w_val, s5_w_gate,
             moe_w_router, moe_w_gate, moe_w_up, moe_w_down, *, n_heads):
    batch, seq, d = x.shape
    ctx_len = ctx.shape[1]
    depth = w_mod.shape[0]
    assert depth == 2
    alpha = (2 * depth) ** 0.25

    cond = jnp.concatenate([c, c_ctx[None, :]], axis=0)
    m = _modulation(cond, w_mod, b_mod)
    x_lat = x.reshape(batch * seq, d)
    x_ctx = ctx.reshape(batch * ctx_len, d)
    lat = dict(rows_per_mod=seq, row0=0)
    cx = dict(rows_per_mod=batch * ctx_len, row0=batch)

    w_qkv = na_w_qkv[0].astype(bf16)
    w_o = na_w_o[0].astype(bf16)
    qkv_lat = _mod_proj(x_lat, m, 0, w_qkv, **lat)
    qkv_ctx = _mod_proj(x_ctx, m, 0, w_qkv, **cx)
    o_lat, o_ctx = _attention(qkv_lat, qkv_ctx, na_rpb[0], batch=batch, seq=seq, ctx_len=ctx_len,
                              n_heads=n_heads)
    wr_t = moe_w_router[0].T
    x1_lat, h2_lat, lg_lat = _mixer_post(o_lat, [w_o], x_lat, m, 0, ln_g[0, 0], ln_b[0, 0], wr_t,
                                         alpha=alpha, **lat)
    x1_ctx, h2_ctx, lg_ctx = _mixer_post(o_ctx, [w_o], x_ctx, m, 0, ln_g[0, 0], ln_b[0, 0], wr_t,
                                         alpha=alpha, **cx)
    f_lat, f_ctx = _moe_layer([h2_lat, h2_ctx], [lg_lat, lg_ctx], [seq, ctx_len], moe_w_gate,
                              moe_w_up, moe_w_down, 0, batch=batch)
    x_lat = _ffn_post(x1_lat, f_lat, m, 0, ln_g[0, 1], ln_b[0, 1], alpha=alpha, **lat)
    x_ctx = _ffn_post(x1_ctx, f_ctx, m, 0, ln_g[0, 1], ln_b[0, 1], alpha=alpha, **cx)

    ar, ai, br, bi = _s5_discretize(s5_lam_re[0], s5_lam_im[0], s5_log_step[0], s5_b_re[0],
                                    s5_b_im[0])
    a_blk, wb_blk, cm_blk = _s5_block_params(ar, ai, br, bi, s5_c_re[0], s5_c_im[0])
    z = _s5_core(x_lat, x_ctx, m, 1, a_blk, wb_blk, cm_blk, s5_d[0], batch=batch, n=seq,
                 nc=ctx_len)
    x1_lat, h2_lat, lg_lat = _mixer_post(z, [s5_w_val[0].astype(bf16), s5_w_gate[0].astype(bf16)],
                                         x_lat, m, 1, ln_g[1, 0], ln_b[1, 0], moe_w_router[1].T,
                                         alpha=alpha, **lat)
    (f_lat,) = _moe_layer([h2_lat], [lg_lat], [seq], moe_w_gate, moe_w_up, moe_w_down, 1,
                          batch=batch)
    x_lat = _ffn_post(x1_lat, f_lat, m, 1, ln_g[1, 1], ln_b[1, 1], alpha=alpha, **lat)
    return x_lat.reshape(batch, seq, d)


def kernel(x, c, ctx, c_ctx, w_mod, b_mod, ln_g, ln_b, na_w_qkv, na_w_o, na_rpb, s5_lam_re,
           s5_lam_im, s5_log_step, s5_b_re, s5_b_im, s5_c_re, s5_c_im, s5_d, s5_w_val, s5_w_gate,
           moe_w_router, moe_w_gate, moe_w_up, moe_w_down):
    return _forward(x, c, ctx, c_ctx, w_mod, b_mod, ln_g, ln_b, na_w_qkv, na_w_o, na_rpb,
                    s5_lam_re, s5_lam_im, s5_log_step, s5_b_re, s5_b_im, s5_c_re, s5_c_im, s5_d,
                    s5_w_val, s5_w_gate, moe_w_router, moe_w_gate, moe_w_up, moe_w_down,
                    n_heads=_N_HEADS)
```

```python
import functools
import math

import jax
import jax.numpy as jnp
from jax import lax
from jax.experimental import pallas as pl
from jax.experimental.pallas import tpu as pltpu

f32 = jnp.float32
bf16 = jnp.bfloat16
i32 = jnp.int32

_GRID_W = 64
_WIN_R = 8
_WIN_C = 16
_N_HEADS = 16
_SSM_GROUP = 16
_CAPACITY_FACTOR = 2
_LN_EPS = 1e-5
_NEG_INF = -1e30

_LANES = 128
_SUBLANES = 8
_VMEM_LIMIT_BYTES = 56 * 1024 * 1024

_NT_DIMS = (((1,), (1,)), ((), ()))


def _cparams(sem):
    return pltpu.CompilerParams(dimension_semantics=sem, vmem_limit_bytes=_VMEM_LIMIT_BYTES)


_MOD_UNROLL = 4


def _mod_body(cb_ref, w_ref, b_ref, o_ref, s_scr, *, n_rows):
    d, tn = w_ref.shape
    reps = tn // _LANES

    @pl.when((pl.program_id(0) == 0) & (pl.program_id(1) == 0))
    def _():
        cv = cb_ref[...]
        s_scr[...] = cv * jax.nn.sigmoid(cv)

    def step(i, accs):
        accs = list(accs)
        for v in range(_MOD_UNROLL):
            k0 = pl.multiple_of((i * _MOD_UNROLL + v) * _SUBLANES, _SUBLANES)
            w = w_ref[pl.ds(k0, _SUBLANES), :]
            for r in range(n_rows):
                s = s_scr[r, pl.ds(k0, _SUBLANES), :]
                accs[r] = accs[r] + jnp.tile(s, (1, reps)) * w
        return tuple(accs)

    init = tuple(jnp.zeros((_SUBLANES, tn), f32) for _ in range(n_rows))
    accs = lax.fori_loop(0, d // (_SUBLANES * _MOD_UNROLL), step, init)
    rows = [jnp.sum(a, axis=0, keepdims=True) + b_ref[...] for a in accs]
    rows.append(jnp.zeros((_SUBLANES - n_rows, tn), f32))
    o_ref[...] = jnp.concatenate(rows, axis=0)


def _modulation(cond, w_mod, b_mod):
    n_rows, d = cond.shape
    depth, _, n6 = w_mod.shape
    tn = min(1024, n6)
    cb = jnp.broadcast_to(cond[:, :, None], (n_rows, d, _LANES))
    return pl.pallas_call(
        functools.partial(_mod_body, n_rows=n_rows),
        grid=(depth, n6 // tn),
        in_specs=[
            pl.BlockSpec((n_rows, d, _LANES), lambda l, j: (0, 0, 0)),
            pl.BlockSpec((None, d, tn), lambda l, j: (l, 0, j)),
            pl.BlockSpec((None, 1, tn), lambda l, j: (l, 0, j)),
        ],
        out_specs=pl.BlockSpec((None, _SUBLANES, tn), lambda l, j: (l, 0, j)),
        out_shape=jax.ShapeDtypeStruct((depth, _SUBLANES, n6), f32),
        scratch_shapes=[pltpu.VMEM((n_rows, d, _LANES), f32)],
        compiler_params=_cparams(("arbitrary", "arbitrary")),
        name="modulation",
    )(cb, w_mod, b_mod.reshape(depth, 1, n6))


def _mod_spec(layer, chunk, d, nargs):
    if nargs == 1:
        return pl.BlockSpec((None, _SUBLANES, d), lambda i: (layer, 0, chunk))
    return pl.BlockSpec((None, _SUBLANES, d), lambda i, j: (layer, 0, chunk))


def _proj_body(x_ref, sc_ref, sh_ref, w_ref, o_ref, h_scr, *, tiles_per_row, row0):
    @pl.when(pl.program_id(1) == 0)
    def _():
        r = row0 + pl.program_id(0) // tiles_per_row
        sc = sc_ref[pl.ds(r, 1), :]
        sh = sh_ref[pl.ds(r, 1), :]
        h_scr[...] = (x_ref[...] * (1.0 + sc) + sh).astype(bf16)

    o_ref[...] = jnp.dot(h_scr[...], w_ref[...], preferred_element_type=f32).astype(o_ref.dtype)


def _mod_proj(x, m, layer, w_bf, *, rows_per_mod, row0):
    mtot, d = x.shape
    n = w_bf.shape[1]
    tm = min(1024, rows_per_mod, mtot)
    tn = min(512, n)
    return pl.pallas_call(
        functools.partial(_proj_body, tiles_per_row=rows_per_mod // tm, row0=row0),
        grid=(mtot // tm, n // tn),
        in_specs=[
            pl.BlockSpec((tm, d), lambda i, j: (i, 0)),
            _mod_spec(layer, 1, d, 2),
            _mod_spec(layer, 0, d, 2),
            pl.BlockSpec((d, tn), lambda i, j: (0, j)),
        ],
        out_specs=pl.BlockSpec((tm, tn), lambda i, j: (i, j)),
        out_shape=jax.ShapeDtypeStruct((mtot, n), bf16),
        scratch_shapes=[pltpu.VMEM((tm, d), bf16)],
        compiler_params=_cparams(("arbitrary", "arbitrary")),
        name="mod_proj",
    )(x, m, m, w_bf)


_ATTN_PAIR_UNROLL = 4
_ATTN_VT_UNROLL = 4
_ATTN_KEY_ROWS = _WIN_R + 2


def _attn_body(rpb_ref, q_ref, k_ref, v_ref, qc_ref, kc_ref, vc_ref, o_ref, oc_ref, t_scr, vt_scr,
               *, rows, dh):
    w = _GRID_W
    n_dr = 2 * _WIN_R - 1
    n_dc = 2 * _WIN_C - 1
    masked = n_dr
    h = pl.program_id(0)
    scale = dh ** -0.5
    lane = lax.broadcasted_iota(i32, (w, 2 * w), 1)
    first_row = lane < w

    @pl.when(pl.program_id(1) == 0)
    def _build():
        kc = lax.broadcasted_iota(i32, (w, 2 * w), 0)
        c = lane & (w - 1)
        delta = jnp.clip(kc - c + (_WIN_C - 1), 0, n_dc - 1)
        cstart = jnp.clip(c - _WIN_C // 2, 0, w - _WIN_C)
        valid = (kc >= cstart) & (kc < cstart + _WIN_C)
        base = h * (n_dr * n_dc)

        def dr_body(dr, carry):
            acc = jnp.zeros((w, 2 * w), f32)
            for d in range(n_dc):
                acc = jnp.where(delta == d, rpb_ref[base + dr * n_dc + d], acc)
            t_scr[dr] = jnp.where(valid, acc, _NEG_INF)
            return carry

        lax.fori_loop(0, n_dr, dr_body, 0)
        t_scr[masked] = jnp.full((w, 2 * w), _NEG_INF, f32)

    def vt_body(i, carry):
        for u in range(_ATTN_VT_UNROLL):
            blk_i = i * _ATTN_VT_UNROLL + u
            blk = v_ref[pl.ds(pl.multiple_of(blk_i * 2 * w, 2 * w), 2 * w), :]
            vt_scr[blk_i] = blk.astype(f32).T.astype(bf16)
        return carry

    lax.fori_loop(0, rows // (2 * _ATTN_VT_UNROLL), vt_body, 0)
    kcx = kc_ref[...]
    vcx = vc_ref[...]
    vct = vcx.astype(f32).T.astype(bf16)

    def pair_scores(i):
        r = 2 * i
        band = jnp.minimum(jnp.clip(r - _WIN_R // 2, 0, rows - _WIN_R - 1) & -2,
                           rows - _ATTN_KEY_ROWS)
        q0 = pl.multiple_of(r * w, 2 * w)
        k0 = pl.multiple_of(band * w, 2 * w)
        q2 = q_ref[pl.ds(q0, 2 * w), :]
        kw = k_ref[pl.ds(k0, _ATTN_KEY_ROWS * w), :]
        blocks = []
        drs = []
        for u in range(2):
            rs = jnp.clip(r + u - _WIN_R // 2, 0, rows - _WIN_R)
            drs.append((rs - (r + u) + (_WIN_R - 1), rs - band))
        for kr in range(_ATTN_KEY_ROWS):
            idx = []
            for s0, off in drs:
                rel = kr - off
                idx.append(jnp.where((rel >= 0) & (rel < _WIN_R), s0 + rel, masked))
            blocks.append(jnp.where(first_row, t_scr[idx[0]], t_scr[idx[1]]))
        bias = jnp.concatenate(blocks, axis=0)
        s_loc = lax.dot_general(kw, q2, _NT_DIMS, preferred_element_type=f32) * scale + bias
        s_ctx = lax.dot_general(kcx, q2, _NT_DIMS, preferred_element_type=f32) * scale
        return s_loc, s_ctx, band, q0

    def pair_output(s_loc, s_ctx, band, q0):
        mx = jnp.maximum(jnp.max(s_loc, axis=0, keepdims=True),
                         jnp.max(s_ctx, axis=0, keepdims=True))
        p_loc = jnp.exp(s_loc - mx)
        p_ctx = jnp.exp(s_ctx - mx)
        den = jnp.sum(p_loc, axis=0, keepdims=True) + jnp.sum(p_ctx, axis=0, keepdims=True)
        vb = band // 2
        vwt = jnp.concatenate([vt_scr[vb + t] for t in range(_ATTN_KEY_ROWS // 2)], axis=1)
        o_t = (jnp.dot(vwt, p_loc.astype(bf16), preferred_element_type=f32)
               + jnp.dot(vct, p_ctx.astype(bf16), preferred_element_type=f32))
        o_ref[pl.ds(q0, 2 * w), :] = (o_t / den).T.astype(o_ref.dtype)

    def pair_body(i, carry):
        scores = [pair_scores(i * _ATTN_PAIR_UNROLL + u) for u in range(_ATTN_PAIR_UNROLL)]
        for sc in scores:
            pair_output(*sc)
        return carry

    lax.fori_loop(0, rows // (2 * _ATTN_PAIR_UNROLL), pair_body, 0)

    s = lax.dot_general(qc_ref[...], kcx, _NT_DIMS, preferred_element_type=f32) * scale
    p = jnp.exp(s - jnp.max(s, axis=1, keepdims=True))
    den = jnp.sum(p, axis=1, keepdims=True)
    oc = jnp.dot(p.astype(bf16), vcx, preferred_element_type=f32)
    oc_ref[...] = (oc / den).astype(oc_ref.dtype)


def _attention(qkv_lat, qkv_ctx, rpb, *, batch, seq, ctx_len, n_heads):
    d = qkv_lat.shape[1] // 3
    dh = d // n_heads
    rows = seq // _GRID_W
    assert dh == _LANES and _GRID_W * 2 == _LANES and rows >= _WIN_R
    assert rows % (2 * _ATTN_PAIR_UNROLL) == 0 and rows >= _ATTN_KEY_ROWS
    assert rows % (2 * _ATTN_VT_UNROLL) == 0
    n_dr = 2 * _WIN_R - 1
    return pl.pallas_call(
        functools.partial(_attn_body, rows=rows, dh=dh),
        grid=(n_heads, batch),
        in_specs=[
            pl.BlockSpec(memory_space=pltpu.SMEM),
            pl.BlockSpec((seq, dh), lambda h, b: (b, h)),
            pl.BlockSpec((seq, dh), lambda h, b: (b, n_heads + h)),
            pl.BlockSpec((seq, dh), lambda h, b: (b, 2 * n_heads + h)),
            pl.BlockSpec((ctx_len, dh), lambda h, b: (b, h)),
            pl.BlockSpec((ctx_len, dh), lambda h, b: (b, n_heads + h)),
            pl.BlockSpec((ctx_len, dh), lambda h, b: (b, 2 * n_heads + h)),
        ],
        out_specs=[
            pl.BlockSpec((seq, dh), lambda h, b: (b, h)),
            pl.BlockSpec((ctx_len, dh), lambda h, b: (b, h)),
        ],
        out_shape=[
            jax.ShapeDtypeStruct((batch * seq, d), bf16),
            jax.ShapeDtypeStruct((batch * ctx_len, d), bf16),
        ],
        scratch_shapes=[
            pltpu.VMEM((n_dr + 1, _GRID_W, 2 * _GRID_W), f32),
            pltpu.VMEM((rows // 2, dh, 2 * _GRID_W), bf16),
        ],
        compiler_params=_cparams(("arbitrary", "arbitrary")),
        name="nbr_attention",
    )(rpb.reshape(-1), qkv_lat, qkv_lat, qkv_lat, qkv_ctx, qkv_ctx, qkv_ctx)


def _layer_norm(y, g, b):
    mu = jnp.mean(y, axis=-1, keepdims=True)
    yc = y - mu
    var = jnp.mean(yc * yc, axis=-1, keepdims=True)
    return yc * lax.rsqrt(var + _LN_EPS) * g + b


def _post_body(*refs, glu, alpha, tiles_per_row, row0):
    if glu:
        a_ref, wv_ref, wg_ref = refs[:3]
        rest = refs[3:]
    else:
        a_ref, wv_ref = refs[:2]
        rest = refs[2:]
    x_ref, g1_ref, sc2_ref, sh2_ref, lng_ref, lnb_ref, wr_ref, x1_ref, h2_ref, lg_ref = rest
    a = a_ref[...]
    o = jnp.dot(a, wv_ref[...], preferred_element_type=f32)
    if glu:
        o = o * jax.nn.sigmoid(jnp.dot(a, wg_ref[...], preferred_element_type=f32))
    r = row0 + pl.program_id(0) // tiles_per_row
    g1 = g1_ref[pl.ds(r, 1), :]
    x1 = _layer_norm(alpha * x_ref[...] + g1 * o, lng_ref[...], lnb_ref[...])
    h2 = x1 * (1.0 + sc2_ref[pl.ds(r, 1), :]) + sh2_ref[pl.ds(r, 1), :]
    x1_ref[...] = x1
    _store_token_tiles(h2_ref, _pack_bf16_pairs(h2))
    n_exp = wr_ref.shape[0]
    wr = wr_ref[...]
    w_hi = wr.astype(bf16)
    w_lo = (wr - w_hi.astype(f32)).astype(bf16)
    h_hi = h2.astype(bf16)
    h_lo = (h2 - h_hi.astype(f32)).astype(bf16)
    both = lax.dot_general(jnp.concatenate([w_hi, w_lo], axis=0), h_hi, _NT_DIMS,
                           preferred_element_type=f32)
    lg_ref[...] = (both[:n_exp] + both[n_exp:]
                   + lax.dot_general(w_hi, h_lo, _NT_DIMS, preferred_element_type=f32))


def _mixer_post(a_bf, weights_bf, x, m, layer, ln_g, ln_b, w_router_t, *, alpha, rows_per_mod,
                row0):
    mtot, d = x.shape
    n_exp = w_router_t.shape[0]
    glu = len(weights_bf) == 2
    tm = min(256 if glu else 512, mtot, rows_per_mod)
    tile_rows = d // 2 // _LANES
    const = lambda i: (0, 0)
    in_specs = [pl.BlockSpec((tm, d), lambda i: (i, 0))]
    in_specs += [pl.BlockSpec((d, d), const, pipeline_mode=pl.Buffered(1)) for _ in weights_bf]
    in_specs += [
        pl.BlockSpec((tm, d), lambda i: (i, 0)),
        _mod_spec(layer, 2, d, 1),
        _mod_spec(layer, 4, d, 1),
        _mod_spec(layer, 3, d, 1),
        pl.BlockSpec((1, d), const),
        pl.BlockSpec((1, d), const),
        pl.BlockSpec((n_exp, d), const),
    ]
    return pl.pallas_call(
        functools.partial(_post_body, glu=glu, alpha=alpha, tiles_per_row=rows_per_mod // tm,
                          row0=row0),
        grid=(mtot // tm,),
        in_specs=in_specs,
        out_specs=[
            pl.BlockSpec((tm, d), lambda i: (i, 0)),
            pl.BlockSpec((tm * tile_rows, _LANES), lambda i: (i, 0)),
            pl.BlockSpec((n_exp, tm), lambda i: (0, i)),
        ],
        out_shape=[
            jax.ShapeDtypeStruct((mtot, d), f32),
            jax.ShapeDtypeStruct((mtot * tile_rows, _LANES), jnp.uint32),
            jax.ShapeDtypeStruct((n_exp, mtot), f32),
        ],
        compiler_params=_cparams(("arbitrary",)),
        name="mixer_post_glu" if glu else "mixer_post",
    )(a_bf, *weights_bf, x, m, m, m, ln_g.reshape(1, d), ln_b.reshape(1, d), w_router_t)


def _cumsum_excl(mask01):
    rows, n = mask01.shape
    r = lax.broadcasted_iota(i32, (_LANES, _LANES), 0)
    c = lax.broadcasted_iota(i32, (_LANES, _LANES), 1)
    tri = jnp.where(r < c, 1.0, 0.0).astype(bf16)
    carry = jnp.zeros((rows, 1), f32)
    outs = []
    for t in range(n // _LANES):
        blk = mask01[:, t * _LANES:(t + 1) * _LANES]
        outs.append(jnp.dot(blk.astype(bf16), tri, preferred_element_type=f32) + carry)
        carry = carry + jnp.sum(blk, axis=1, keepdims=True)
    return outs[0] if len(outs) == 1 else jnp.concatenate(outs, axis=1)


def _route_body(lg_ref, idx_ref, gate_ref, posm_scr, aff_scr, *, n, cap, chunk):
    n_exp = lg_ref.shape[0]
    x = lg_ref[...]
    ex = jnp.exp(x - jnp.max(x, axis=0, keepdims=True))
    aff = ex / jnp.sum(ex, axis=0, keepdims=True)
    bits = pltpu.bitcast(aff, i32)

    thr = jnp.zeros((n_exp, 1), i32)
    for bit in range(30, -1, -1):
        cand = thr | (1 << bit)
        cnt = jnp.sum(jnp.where(bits >= cand, 1.0, 0.0), axis=1, keepdims=True)
        thr = jnp.where(cnt >= cap, cand, thr)
    gt = bits > thr
    eq = bits == thr
    need = cap - jnp.sum(jnp.where(gt, 1.0, 0.0), axis=1, keepdims=True)
    rank_eq = _cumsum_excl(jnp.where(eq, 1.0, 0.0))
    sel = gt | (eq & (rank_eq < need))
    pos = _cumsum_excl(jnp.where(sel, 1.0, 0.0))
    posm_scr[...] = jnp.where(sel, pos, -1.0)
    a_hi = aff.astype(bf16)
    r1 = aff - a_hi.astype(f32)
    a_mid = r1.astype(bf16)
    a_lo = (r1 - a_mid.astype(f32)).astype(bf16)
    for k, piece in enumerate((a_hi, a_mid, a_lo)):
        aff_scr[k] = piece.astype(f32)
    tok = lax.broadcasted_iota(i32, (1, n), 1)
    digits = jnp.concatenate([(tok >> 6).astype(f32), (tok & 63).astype(f32)], axis=0)
    pad = jnp.zeros((_SUBLANES - 5, n), f32)
    tok_base = pl.program_id(0) * n

    def per_expert(e, carry):
        prow = posm_scr[pl.ds(e, 1), :]
        rows = [digits] + [aff_scr[k, pl.ds(e, 1), :] for k in range(3)] + [pad]
        vals = jnp.concatenate(rows, axis=0).astype(bf16)
        outs = []
        for pc in range(cap // chunk):
            slot = (lax.broadcasted_iota(i32, (chunk, n), 0) + pc * chunk).astype(f32)
            hit = jnp.where(prow == slot, 1.0, 0.0).astype(bf16)
            outs.append(lax.dot_general(vals, hit, _NT_DIMS, preferred_element_type=f32))
        got = outs[0] if len(outs) == 1 else jnp.concatenate(outs, axis=1)
        idx = got[0:1] * 64.0 + got[1:2]
        idx_ref[pl.ds(e, 1), :] = idx.astype(i32) + tok_base
        gate_ref[pl.ds(e, 1), :] = got[2:3] + got[3:4] + got[4:5]
        return carry

    lax.fori_loop(0, n_exp, per_expert, 0)


def _route(logits_t, *, batch, n):
    n_exp = logits_t.shape[0]
    cap = _CAPACITY_FACTOR * n // n_exp
    chunk = min(_LANES, cap)
    assert n % _LANES == 0 and cap % chunk == 0 and n <= 64 * 64
    return pl.pallas_call(
        functools.partial(_route_body, n=n, cap=cap, chunk=chunk),
        grid=(batch,),
        in_specs=[pl.BlockSpec((n_exp, n), lambda b: (0, b))],
        out_specs=[
            pl.BlockSpec((None, n_exp, cap), lambda b: (b, 0, 0)),
            pl.BlockSpec((None, n_exp, cap), lambda b: (b, 0, 0)),
        ],
        out_shape=[
            jax.ShapeDtypeStruct((batch, n_exp, cap), i32),
            jax.ShapeDtypeStruct((batch, n_exp, cap), f32),
        ],
        scratch_shapes=[pltpu.VMEM((n_exp, n), f32), pltpu.VMEM((3, n_exp, n), f32)],
        compiler_params=_cparams(("arbitrary",)),
        name="route_topk",
    )(logits_t)


def _pack_bf16_pairs(h):
    half = h.shape[1] // 2
    u = pltpu.bitcast(h.astype(bf16).astype(f32), jnp.uint32)
    return (u[:, :half] & jnp.uint32(0xFFFF0000)) | (u[:, half:] >> 16)


def _store_token_tiles(ref, packed):
    m, width = packed.shape
    r = width // _LANES
    for s in range(r):
        ref[pl.ds(s, m, stride=r), :] = packed[:, s * _LANES:(s + 1) * _LANES]


def _load_token_tiles(ref, m):
    r = ref.shape[0] // m
    chunks = [ref[pl.ds(s, m, stride=r), :] for s in range(r)]
    return jnp.concatenate(chunks, axis=1)


def _unpack_bf16_pairs(p):
    hi = pltpu.bitcast(p & jnp.uint32(0xFFFF0000), f32).astype(bf16)
    lo = pltpu.bitcast(p << 16, f32).astype(bf16)
    return jnp.concatenate([hi, lo], axis=1)


_GATHER_UNROLL = 8


def _moe_body(idx_ref, *refs, seg_counts, m_slots, n_tiles):
    n_src = len(seg_counts)
    srcs = refs[:n_src]
    wg_ref, wu_ref, wd_ref, o_ref, xp_scr, xe16, hmid, sem = refs[n_src:]
    e = pl.program_id(0)
    j = pl.program_id(1)
    tf = wg_ref.shape[1]

    tile_rows = xp_scr.shape[0] // m_slots

    def tile_copy(src, row, slot):
        r0 = pl.multiple_of(row * tile_rows, tile_rows)
        s0 = pl.multiple_of(slot * tile_rows, tile_rows)
        return pltpu.make_async_copy(src.at[pl.ds(r0, tile_rows), :],
                                     xp_scr.at[pl.ds(s0, tile_rows), :], sem)

    def for_each_slot(fn):
        slot0 = 0
        for src, cnt in zip(srcs, seg_counts):
            def body(i, c, src=src, slot0=slot0):
                for u in range(_GATHER_UNROLL):
                    fn(src, slot0 + i * _GATHER_UNROLL + u)
                return c

            lax.fori_loop(0, cnt // _GATHER_UNROLL, body, 0)
            slot0 += cnt

    def issue_gather(expert):
        base = expert * m_slots
        for_each_slot(lambda src, slot: tile_copy(src, idx_ref[base + slot], slot).start())

    n_exp = pl.num_programs(0)
    last_step = 2 * n_tiles - 1

    @pl.when(j == 0)
    def _rows():
        @pl.when(e == 0)
        def _():
            issue_gather(0)

        for_each_slot(lambda src, slot: tile_copy(src, 0, slot).wait())
        xe16[...] = _unpack_bf16_pairs(_load_token_tiles(xp_scr, m_slots))

    @pl.when(j < n_tiles)
    def _up():
        x = xe16[...]
        a = jnp.dot(x, wg_ref[...].astype(bf16), preferred_element_type=f32)
        u = jnp.dot(x, wu_ref[...].astype(bf16), preferred_element_type=f32)
        hmid[j] = (a * jax.nn.sigmoid(a) * u).astype(bf16)

    slot_src = [src for src, cnt in zip(srcs, seg_counts) for _ in range(cnt)]
    per_step = m_slots // n_tiles
    next_base = jnp.minimum(e + 1, n_exp - 1) * m_slots
    for t in range(n_tiles):
        @pl.when(j == n_tiles + t)
        def _down(t=t):
            for slot in range(t * per_step, (t + 1) * per_step):
                tile_copy(slot_src[slot], idx_ref[next_base + slot], slot).start()
            y = jnp.dot(hmid[0], wd_ref[0:tf, :].astype(bf16), preferred_element_type=f32)
            for k in range(1, n_tiles):
                y += jnp.dot(hmid[k], wd_ref[k * tf:(k + 1) * tf, :].astype(bf16),
                             preferred_element_type=f32)
            o_ref[...] = y

    @pl.when((e == n_exp - 1) & (j == last_step))
    def _drain():
        for_each_slot(lambda src, slot: tile_copy(src, 0, slot).wait())


def _moe_experts(idx_flat, srcs, seg_counts, w_gate, w_up, w_down, layer):
    _, n_exp, d, ff = w_gate.shape
    m_slots = sum(seg_counts)
    tf = min(512, ff)
    n_tiles = ff // tf
    assert d // tf == n_tiles and all(c % _GATHER_UNROLL == 0 for c in seg_counts)
    assert m_slots % n_tiles == 0
    n_src = len(srcs)
    up_idx = lambda e, j, idx: (layer, e, 0, jnp.minimum(j, n_tiles - 1))
    dn_idx = lambda e, j, idx: (layer, e, 0, jnp.maximum(j - n_tiles, 0))
    grid_spec = pltpu.PrefetchScalarGridSpec(
        num_scalar_prefetch=1,
        grid=(n_exp, 2 * n_tiles),
        in_specs=[pl.BlockSpec(memory_space=pl.ANY)] * n_src + [
            pl.BlockSpec((None, None, d, tf), up_idx),
            pl.BlockSpec((None, None, d, tf), up_idx),
            pl.BlockSpec((None, None, ff, tf), dn_idx),
        ],
        out_specs=pl.BlockSpec((None, m_slots, tf),
                               lambda e, j, idx: (e, 0, jnp.maximum(j - n_tiles, 0))),
        scratch_shapes=[
            pltpu.VMEM((m_slots * (d // 2 // _LANES), _LANES), jnp.uint32),
            pltpu.VMEM((m_slots, d), bf16),
            pltpu.VMEM((n_tiles, m_slots, tf), bf16),
            pltpu.SemaphoreType.DMA(()),
        ],
    )
    return pl.pallas_call(
        functools.partial(_moe_body, seg_counts=tuple(seg_counts), m_slots=m_slots,
                          n_tiles=n_tiles),
        grid_spec=grid_spec,
        out_shape=jax.ShapeDtypeStruct((n_exp, m_slots, d), f32),
        compiler_params=_cparams(("arbitrary", "arbitrary")),
        name="moe_experts",
    )(idx_flat, *srcs, w_gate, w_up, w_down)


_SCATTER_UNROLL = 2


_POST_ROWS = 256


def _combine_body(idx_ref, gate_ref, ye_ref, g2_ref, lng_ref, lnb_ref, x1_hbm, o_hbm, acc, xbuf,
                  obuf, sem_in, sem_out, *, m_slots, slot_base, cap, n, alpha, row0, row_stride):
    b = pl.program_id(0)
    e = pl.program_id(1)

    @pl.when(e == 0)
    def _():
        acc[...] = jnp.zeros_like(acc)

    base = e * m_slots + slot_base + b * cap
    tok0 = b * n

    def body(i, carry):
        j0 = i * _SCATTER_UNROLL
        toks = [idx_ref[base + j0 + u] - tok0 for u in range(_SCATTER_UNROLL)]
        vals = [acc[pl.ds(toks[u], 1), :] + gate_ref[base + j0 + u] * ye_ref[pl.ds(j0 + u, 1), :]
                for u in range(_SCATTER_UNROLL)]
        for u in range(_SCATTER_UNROLL):
            acc[pl.ds(toks[u], 1), :] = vals[u]
        return carry

    lax.fori_loop(0, cap // _SCATTER_UNROLL, body, 0)

    @pl.when(e == pl.num_programs(1) - 1)
    def _post():
        tr = xbuf.shape[1]
        n_t = n // tr
        g2 = g2_ref[pl.ds(row0 + b * row_stride, 1), :]

        def rows(t):
            return pl.ds(pl.multiple_of(b * n + t * tr, tr), tr)

        def x_copy(t):
            return pltpu.make_async_copy(x1_hbm.at[rows(t), :], xbuf.at[t % 2], sem_in.at[t % 2])

        def o_copy(t):
            return pltpu.make_async_copy(obuf.at[t % 2], o_hbm.at[rows(t), :], sem_out.at[t % 2])

        x_copy(0).start()
        for t in range(n_t):
            if t + 1 < n_t:
                x_copy(t + 1).start()
            x_copy(t).wait()
            if t >= 2:
                o_copy(t - 2).wait()
            f = acc[t * tr:(t + 1) * tr, :]
            obuf[t % 2] = _layer_norm(alpha * xbuf[t % 2] + g2 * f, lng_ref[...], lnb_ref[...])
            o_copy(t).start()
        for t in range(max(n_t - 2, 0), n_t):
            o_copy(t).wait()


def _combine_post(idx_flat, gate_flat, ye, x1, m, layer, ln_g, ln_b, *, batch, n, cap, slot_base,
                  alpha, row0, row_stride):
    n_exp, m_slots, d = ye.shape
    assert slot_base % cap == 0 and cap % _SCATTER_UNROLL == 0
    blk0 = slot_base // cap
    tr = min(_POST_ROWS, n)
    const = lambda b, e, idx, gate: (0, 0)
    grid_spec = pltpu.PrefetchScalarGridSpec(
        num_scalar_prefetch=2,
        grid=(batch, n_exp),
        in_specs=[
            pl.BlockSpec((None, cap, d), lambda b, e, idx, gate: (e, blk0 + b, 0)),
            pl.BlockSpec((None, _SUBLANES, d), lambda b, e, idx, gate: (layer, 0, 5)),
            pl.BlockSpec((1, d), const),
            pl.BlockSpec((1, d), const),
            pl.BlockSpec(memory_space=pl.ANY),
        ],
        out_specs=pl.BlockSpec(memory_space=pl.ANY),
        scratch_shapes=[
            pltpu.VMEM((n, d), f32),
            pltpu.VMEM((2, tr, d), f32),
            pltpu.VMEM((2, tr, d), f32),
            pltpu.SemaphoreType.DMA((2,)),
            pltpu.SemaphoreType.DMA((2,)),
        ],
    )
    return pl.pallas_call(
        functools.partial(_combine_body, m_slots=m_slots, slot_base=slot_base, cap=cap, n=n,
                          alpha=alpha, row0=row0, row_stride=row_stride),
        grid_spec=grid_spec,
        out_shape=jax.ShapeDtypeStruct((batch * n, d), f32),
        compiler_params=_cparams(("arbitrary", "arbitrary")),
        name="moe_combine",
    )(idx_flat, gate_flat, ye, m, ln_g.reshape(1, d), ln_b.reshape(1, d), x1)


def _disc_body(lr_ref, li_ref, ls_ref, bre_ref, bim_ref, ar_ref, ai_ref, br_ref, bi_ref):
    lr = jnp.minimum(lr_ref[...], -1e-4)
    li = li_ref[...]
    dt = jnp.exp(ls_ref[...])
    mag = jnp.exp(lr * dt)
    ar = mag * jnp.cos(li * dt)
    ai = mag * jnp.sin(li * dt)
    nr = ar - 1.0
    den = lr * lr + li * li
    cr = (nr * lr + ai * li) / den
    ci = (ai * lr - nr * li) / den
    ar_ref[...] = ar
    ai_ref[...] = ai
    br_ref[...] = cr * bre_ref[...] - ci * bim_ref[...]
    bi_ref[...] = cr * bim_ref[...] + ci * bre_ref[...]


def _s5_discretize(lam_re, lam_im, log_step, b_re, b_im):
    shape = b_re.shape
    flat = (shape[0] * shape[1], shape[2] * shape[3])
    bc = lambda a: jnp.broadcast_to(a, shape).reshape(flat)
    outs = pl.pallas_call(
        _disc_body,
        out_shape=[jax.ShapeDtypeStruct(flat, f32)] * 4,
        name="s5_discretize",
    )(bc(lam_re[..., None]), bc(lam_im[..., None]), bc(log_step[:, :, None, None]),
      b_re.reshape(flat), b_im.reshape(flat))
    ar, ai, br, bi = [o.reshape(shape) for o in outs]
    return ar[..., 0], ai[..., 0], br, bi


_SCAN_UNROLL = 8


def _cmul(ar, ai, xr, xi):
    return ar * xr - ai * xi, ar * xi + ai * xr


def _cpow(ar, ai, k):
    rr, ri = jnp.ones_like(ar), jnp.zeros_like(ai)
    while k:
        if k & 1:
            rr, ri = _cmul(ar, ai, rr, ri)
        ar, ai = _cmul(ar, ai, ar, ai)
        k >>= 1
    return rr, ri


def _segment_scan(bu_ref, seg_len, s_dim, ar, ai, init, reverse, xs_ref=None):
    ns = _SUBLANES
    unroll = math.gcd(seg_len, _SCAN_UNROLL)
    assert xs_ref is None or unroll % 2 == 0

    def outer(jo, carry):
        xr, xi = carry
        prev = None
        for ji in range(unroll):
            jj = jo * unroll + ji
            j = (seg_len - 1 - jj) if reverse else jj
            r0 = pl.multiple_of(j * ns, ns)
            br = bu_ref[pl.ds(r0, ns), 0:s_dim]
            bi = bu_ref[pl.ds(r0, ns), s_dim:2 * s_dim]
            xr, xi = ar * xr - ai * xi + br, ar * xi + ai * xr + bi
            if xs_ref is not None:
                if ji % 2 == 0:
                    prev = (xr, xi)
                else:
                    pair = ((xr, prev[0]), (xi, prev[1])) if reverse else ((prev[0], xr),
                                                                          (prev[1], xi))
                    p0 = pl.multiple_of((j if reverse else j - 1) * ns, 2 * ns)
                    for half, (lo, hi) in enumerate(pair):
                        xs_ref[pl.ds(p0, 2 * ns), half * s_dim:(half + 1) * s_dim] = (
                            jnp.concatenate([lo, hi], axis=0).astype(bf16))
        return xr, xi

    return lax.fori_loop(0, seg_len // unroll, outer, init)


def _segment_inits(ends, a_len, h0, reverse):
    er, ei = ends
    alr, ali = a_len
    ns = _SUBLANES
    order = range(ns - 1, -1, -1) if reverse else range(ns)
    cr, ci = h0
    inits_r, inits_i = [None] * ns, [None] * ns
    for s in order:
        inits_r[s], inits_i[s] = cr, ci
        pr, pi = _cmul(alr, ali, cr, ci)
        cr, ci = pr + er[s:s + 1, :], pi + ei[s:s + 1, :]
    return (jnp.concatenate(inits_r, axis=0), jnp.concatenate(inits_i, axis=0)), (cr, ci)


def _s5_body(x_ref, xc_ref, sh_ref, sc_ref, a_ref, wb_ref, cm_ref, d_ref, z_ref,
             up_scr, ucp_scr, bu_scr, xs_scr, buc_scr, y_scr, yn_scr, *, n, nc, batch):
    ns = _SUBLANES
    seg, segc = n // ns, nc // ns
    s_dim = a_ref.shape[-1]
    b = pl.program_id(0)
    sc_l, sh_l = sc_ref[pl.ds(b, 1), :], sh_ref[pl.ds(b, 1), :]
    sc_c, sh_c = sc_ref[batch:batch + 1, :], sh_ref[batch:batch + 1, :]

    for j in range(seg):
        up_scr[j * ns:(j + 1) * ns, :] = x_ref[pl.ds(j, ns, stride=seg), :] * (1.0 + sc_l) + sh_l
    for j in range(segc):
        ucp_scr[j * ns:(j + 1) * ns, :] = xc_ref[pl.ds(j, ns, stride=segc), :] * (1.0 + sc_c) + sh_c

    u = up_scr[...]
    y_scr[...] = d_ref[...] * u
    uc_bf = ucp_scr[...].astype(bf16)
    zero = (jnp.zeros((ns, s_dim), f32), jnp.zeros((ns, s_dim), f32))
    u_bf = u.astype(bf16)
    for direction in range(2):
        reverse = direction == 1
        ar1, ai1 = a_ref[direction, 0:1, :], a_ref[direction, 1:2, :]
        ar = jnp.broadcast_to(ar1, (ns, s_dim))
        ai = jnp.broadcast_to(ai1, (ns, s_dim))
        buc_scr[...] = jnp.dot(uc_bf, wb_ref[direction], preferred_element_type=f32)
        ends_c = _segment_scan(buc_scr, segc, s_dim, ar, ai, zero, reverse)
        zero1 = (jnp.zeros((1, s_dim), f32), jnp.zeros((1, s_dim), f32))
        _, h0 = _segment_inits(ends_c, _cpow(ar1, ai1, segc), zero1, reverse)
        for lo in (0, n // 2):
            bu_scr[lo:lo + n // 2, :] = jnp.dot(u_bf[lo:lo + n // 2], wb_ref[direction],
                                                preferred_element_type=f32)
        ends = _segment_scan(bu_scr, seg, s_dim, ar, ai, zero, reverse)
        inits, _ = _segment_inits(ends, _cpow(ar1, ai1, seg), h0, reverse)
        _segment_scan(bu_scr, seg, s_dim, ar, ai, inits, reverse, xs_ref=xs_scr)
        for lo in (0, n // 2):
            y_scr[lo:lo + n // 2, :] += jnp.dot(xs_scr[lo:lo + n // 2, :], cm_ref[direction],
                                                preferred_element_type=f32)

    for j in range(seg):
        yn_scr[pl.ds(j, ns, stride=seg), :] = y_scr[j * ns:(j + 1) * ns, :]
    z_ref[...] = jax.nn.gelu(yn_scr[...]).astype(z_ref.dtype)


def _s5_core(x_lat, x_ctx, m, layer, a_blk, wb_blk, cm_blk, d_skip, *, batch, n, nc):
    d = x_lat.shape[1]
    n_blk = d // _LANES
    s_dim = a_blk.shape[-1]
    assert n % (_SUBLANES * _SCAN_UNROLL) == 0 and nc % _SUBLANES == 0
    return pl.pallas_call(
        functools.partial(_s5_body, n=n, nc=nc, batch=batch),
        grid=(batch, n_blk),
        in_specs=[
            pl.BlockSpec((n, _LANES), lambda b, g: (b, g)),
            pl.BlockSpec((nc, _LANES), lambda b, g: (b, g)),
            pl.BlockSpec((None, _SUBLANES, _LANES), lambda b, g: (layer, 0, g)),
            pl.BlockSpec((None, _SUBLANES, _LANES), lambda b, g: (layer, 0, n_blk + g)),
            pl.BlockSpec((2, None, 2, s_dim), lambda b, g: (0, g, 0, 0)),
            pl.BlockSpec((2, None, _LANES, 2 * s_dim), lambda b, g: (0, g, 0, 0)),
            pl.BlockSpec((2, None, 2 * s_dim, _LANES), lambda b, g: (0, g, 0, 0)),
            pl.BlockSpec((1, _LANES), lambda b, g: (0, g)),
        ],
        out_specs=pl.BlockSpec((n, _LANES), lambda b, g: (b, g)),
        out_shape=jax.ShapeDtypeStruct((batch * n, d), bf16),
        scratch_shapes=[
            pltpu.VMEM((n, _LANES), f32),
            pltpu.VMEM((nc, _LANES), f32),
            pltpu.VMEM((n, 2 * s_dim), f32),
            pltpu.VMEM((n, 2 * s_dim), bf16),
            pltpu.VMEM((nc, 2 * s_dim), f32),
            pltpu.VMEM((n, _LANES), f32),
            pltpu.VMEM((n, _LANES), f32),
        ],
        compiler_params=_cparams(("arbitrary", "arbitrary")),
        name="s5_core",
    )(x_lat, x_ctx, m, m, a_blk, wb_blk, cm_blk, d_skip.reshape(1, d))


def _s5_block_params(ar, ai, br, bi, c_re, c_im):
    _, n_groups, p_dim, gh = br.shape
    gpb = _LANES // gh
    n_blk = n_groups // gpb
    s_dim = gpb * p_dim
    eye = jnp.eye(gpb, dtype=f32)
    a_blk = jnp.stack([ar.reshape(2, n_blk, s_dim), ai.reshape(2, n_blk, s_dim)], axis=2)

    def b_mat(bx):
        t = bx.reshape(2, n_blk, gpb, p_dim, gh)
        return jnp.einsum("dngph,gk->dnghkp", t, eye).reshape(2, n_blk, gpb * gh, s_dim)

    def c_mat(cx):
        t = cx.reshape(2, n_blk, gpb, gh, p_dim)
        return jnp.einsum("dnghp,gk->dngpkh", t, eye).reshape(2, n_blk, s_dim, gpb * gh)

    wb = jnp.concatenate([b_mat(br), b_mat(bi)], axis=-1).astype(bf16)
    cm = jnp.concatenate([c_mat(c_re), -c_mat(c_im)], axis=-2).astype(bf16)
    return a_blk, wb, cm


def _moe_layer(h2_list, logits_list, x1_list, dims, mod_rows, w_gate, w_up, w_down, m, layer,
               ln_g, ln_b, *, batch, alpha):
    idxs, gates, caps = [], [], []
    for lg, n in zip(logits_list, dims):
        idx, gate = _route(lg, batch=batch, n=n)
        idxs.append(idx)
        gates.append(gate)
        caps.append(idx.shape[-1])
    n_exp = w_gate.shape[1]
    idx_all = jnp.concatenate([jnp.swapaxes(i, 0, 1).reshape(n_exp, -1) for i in idxs], axis=1)
    gate_all = jnp.concatenate([jnp.swapaxes(g, 0, 1).reshape(n_exp, -1) for g in gates], axis=1)
    seg_counts = [batch * c for c in caps]
    idx_flat = idx_all.reshape(-1)
    gate_flat = gate_all.reshape(-1)
    ye = _moe_experts(idx_flat, h2_list, seg_counts, w_gate, w_up, w_down, layer)
    outs, slot_base = [], 0
    for n, cap, x1, (row0, row_stride) in zip(dims, caps, x1_list, mod_rows):
        outs.append(_combine_post(idx_flat, gate_flat, ye, x1, m, layer, ln_g, ln_b, batch=batch,
                                  n=n, cap=cap, slot_base=slot_base, alpha=alpha, row0=row0,
                                  row_stride=row_stride))
        slot_base += batch * cap
    return outs


def _forward(x, c, ctx, c_ctx, w_mod, b_mod, ln_g, ln_b, na_w_qkv, na_w_o, na_rpb, s5_lam_re,
             s5_lam_im, s5_log_step, s5_b_re, s5_b_im, s5_c_re, s5_c_im, s5_d, s5_w_val, s5_w_gate,
             moe_w_router, moe_w_gate, moe_w_up, moe_w_down, *, n_heads):
    batch, seq, d = x.shape
    ctx_len = ctx.shape[1]
    depth = w_mod.shape[0]
    assert depth == 2
    alpha = (2 * depth) ** 0.25

    cond = jnp.concatenate([c, c_ctx[None, :]], axis=0)
    m = _modulation(cond, w_mod, b_mod)
    x_lat = x.reshape(batch * seq, d)
    x_ctx = ctx.reshape(batch * ctx_len, d)
    lat = dict(rows_per_mod=seq, row0=0)
    cx = dict(rows_per_mod=batch * ctx_len, row0=batch)

    w_qkv = na_w_qkv[0].astype(bf16)
    w_o = na_w_o[0].astype(bf16)
    qkv_lat = _mod_proj(x_lat, m, 0, w_qkv, **lat)
    qkv_ctx = _mod_proj(x_ctx, m, 0, w_qkv, **cx)
    o_lat, o_ctx = _attention(qkv_lat, qkv_ctx, na_rpb[0], batch=batch, seq=seq, ctx_len=ctx_len,
                              n_heads=n_heads)
    wr_t = moe_w_router[0].T
    x1_lat, h2_lat, lg_lat = _mixer_post(o_lat, [w_o], x_lat, m, 0, ln_g[0, 0], ln_b[0, 0], wr_t,
                                         alpha=alpha, **lat)
    x1_ctx, h2_ctx, lg_ctx = _mixer_post(o_ctx, [w_o], x_ctx, m, 0, ln_g[0, 0], ln_b[0, 0], wr_t,
                                         alpha=alpha, **cx)
    lat_rows, ctx_rows = (0, 1), (batch, 0)
    x_lat, x_ctx = _moe_layer([h2_lat, h2_ctx], [lg_lat, lg_ctx], [x1_lat, x1_ctx],
                              [seq, ctx_len], [lat_rows, ctx_rows], moe_w_gate, moe_w_up,
                              moe_w_down, m, 0, ln_g[0, 1], ln_b[0, 1], batch=batch, alpha=alpha)

    ar, ai, br, bi = _s5_discretize(s5_lam_re[0], s5_lam_im[0], s5_log_step[0], s5_b_re[0],
                                    s5_b_im[0])
    a_blk, wb_blk, cm_blk = _s5_block_params(ar, ai, br, bi, s5_c_re[0], s5_c_im[0])
    z = _s5_core(x_lat, x_ctx, m, 1, a_blk, wb_blk, cm_blk, s5_d[0], batch=batch, n=seq,
                 nc=ctx_len)
    x1_lat, h2_lat, lg_lat = _mixer_post(z, [s5_w_val[0].astype(bf16), s5_w_gate[0].astype(bf16)],
                                         x_lat, m, 1, ln_g[1, 0], ln_b[1, 0], moe_w_router[1].T,
                                         alpha=alpha, **lat)
    (x_lat,) = _moe_layer([h2_lat], [lg_lat], [x1_lat], [seq], [lat_rows], moe_w_gate, moe_w_up,
                          moe_w_down, m, 1, ln_g[1, 1], ln_b[1, 1], batch=batch, alpha=alpha)
    return x_lat.reshape(batch, seq, d)


def kernel(x, c, ctx, c_ctx, w_mod, b_mod, ln_g, ln_b, na_w_qkv, na_w_o, na_rpb, s5_lam_re,
           s5_lam_im, s5_log_step, s5_b_re, s5_b_im, s5_c_re, s5_c_im, s5_d, s5_w_val, s5_w_gate,
           moe_w_router, moe_w_gate, moe_w_up, moe_w_down):
    return _forward(x, c, ctx, c_ctx, w_mod, b_mod, ln_g, ln_b, na_w_qkv, na_w_o, na_rpb,
                    s5_lam_re, s5_lam_im, s5_log_step, s5_b_re, s5_b_im, s5_c_re, s5_c_im, s5_d,
                    s5_w_val, s5_w_gate, moe_w_router, moe_w_gate, moe_w_up, moe_w_down,
                    n_heads=_N_HEADS)
```

```python
import functools
import math

import jax
import jax.numpy as jnp
from jax import lax
from jax.experimental import pallas as pl
from jax.experimental.pallas import tpu as pltpu

f32 = jnp.float32
bf16 = jnp.bfloat16
i32 = jnp.int32

_GRID_W = 64
_WIN_R = 8
_WIN_C = 16
_N_HEADS = 16
_SSM_GROUP = 16
_CAPACITY_FACTOR = 2
_LN_EPS = 1e-5
_NEG_INF = -1e30

_LANES = 128
_SUBLANES = 8
_VMEM_LIMIT_BYTES = 56 * 1024 * 1024

_NT_DIMS = (((1,), (1,)), ((), ()))


def _cparams(sem):
    return pltpu.CompilerParams(dimension_semantics=sem, vmem_limit_bytes=_VMEM_LIMIT_BYTES)


_MOD_UNROLL = 4


def _mod_body(cb_ref, w_ref, b_ref, o_ref, s_scr, *, n_rows):
    d, tn = w_ref.shape
    reps = tn // _LANES

    @pl.when((pl.program_id(0) == 0) & (pl.program_id(1) == 0))
    def _():
        cv = cb_ref[...]
        s_scr[...] = cv * jax.nn.sigmoid(cv)

    def step(i, accs):
        accs = list(accs)
        for v in range(_MOD_UNROLL):
            k0 = pl.multiple_of((i * _MOD_UNROLL + v) * _SUBLANES, _SUBLANES)
            w = w_ref[pl.ds(k0, _SUBLANES), :]
            for r in range(n_rows):
                s = s_scr[r, pl.ds(k0, _SUBLANES), :]
                accs[r] = accs[r] + jnp.tile(s, (1, reps)) * w
        return tuple(accs)

    init = tuple(jnp.zeros((_SUBLANES, tn), f32) for _ in range(n_rows))
    accs = lax.fori_loop(0, d // (_SUBLANES * _MOD_UNROLL), step, init)
    rows = [jnp.sum(a, axis=0, keepdims=True) + b_ref[...] for a in accs]
    rows.append(jnp.zeros((_SUBLANES - n_rows, tn), f32))
    o_ref[...] = jnp.concatenate(rows, axis=0)


def _modulation(cond, w_mod, b_mod):
    n_rows, d = cond.shape
    depth, _, n6 = w_mod.shape
    tn = min(1024, n6)
    cb = jnp.broadcast_to(cond[:, :, None], (n_rows, d, _LANES))
    return pl.pallas_call(
        functools.partial(_mod_body, n_rows=n_rows),
        grid=(depth, n6 // tn),
        in_specs=[
            pl.BlockSpec((n_rows, d, _LANES), lambda l, j: (0, 0, 0)),
            pl.BlockSpec((None, d, tn), lambda l, j: (l, 0, j)),
            pl.BlockSpec((None, 1, tn), lambda l, j: (l, 0, j)),
        ],
        out_specs=pl.BlockSpec((None, _SUBLANES, tn), lambda l, j: (l, 0, j)),
        out_shape=jax.ShapeDtypeStruct((depth, _SUBLANES, n6), f32),
        scratch_shapes=[pltpu.VMEM((n_rows, d, _LANES), f32)],
        compiler_params=_cparams(("arbitrary", "arbitrary")),
        name="modulation",
    )(cb, w_mod, b_mod.reshape(depth, 1, n6))


def _mod_spec(layer, chunk, d, nargs):
    if nargs == 1:
        return pl.BlockSpec((None, _SUBLANES, d), lambda i: (layer, 0, chunk))
    return pl.BlockSpec((None, _SUBLANES, d), lambda i, j: (layer, 0, chunk))


def _proj_body(x_ref, sc_ref, sh_ref, w_ref, o_ref, h_scr, *, tiles_per_row, row0):
    @pl.when(pl.program_id(1) == 0)
    def _():
        r = row0 + pl.program_id(0) // tiles_per_row
        sc = sc_ref[pl.ds(r, 1), :]
        sh = sh_ref[pl.ds(r, 1), :]
        h_scr[...] = (x_ref[...] * (1.0 + sc) + sh).astype(bf16)

    o_ref[...] = jnp.dot(h_scr[...], w_ref[...], preferred_element_type=f32).astype(o_ref.dtype)


def _mod_proj(x, m, layer, w_bf, *, rows_per_mod, row0):
    mtot, d = x.shape
    n = w_bf.shape[1]
    tm = min(1024, rows_per_mod, mtot)
    tn = min(512, n)
    return pl.pallas_call(
        functools.partial(_proj_body, tiles_per_row=rows_per_mod // tm, row0=row0),
        grid=(mtot // tm, n // tn),
        in_specs=[
            pl.BlockSpec((tm, d), lambda i, j: (i, 0)),
            _mod_spec(layer, 1, d, 2),
            _mod_spec(layer, 0, d, 2),
            pl.BlockSpec((d, tn), lambda i, j: (0, j)),
        ],
        out_specs=pl.BlockSpec((tm, tn), lambda i, j: (i, j)),
        out_shape=jax.ShapeDtypeStruct((mtot, n), bf16),
        scratch_shapes=[pltpu.VMEM((tm, d), bf16)],
        compiler_params=_cparams(("arbitrary", "arbitrary")),
        name="mod_proj",
    )(x, m, m, w_bf)


_ATTN_HEADS = 2
_ATTN_PAIR_UNROLL = 4
_ATTN_VT_UNROLL = 4
_ATTN_KEY_ROWS = _WIN_R + 2


def _attn_body(rpb_ref, q_ref, k_ref, v_ref, qc_ref, kc_ref, vc_ref, o_ref, oc_ref, t_scr, vt_scr,
               *, rows, dh):
    w = _GRID_W
    n_dr = 2 * _WIN_R - 1
    n_dc = 2 * _WIN_C - 1
    masked = n_dr
    scale = dh ** -0.5
    lane = lax.broadcasted_iota(i32, (w, 2 * w), 1)
    first_row = lane < w

    @pl.when(pl.program_id(1) == 0)
    def _build():
        kc = lax.broadcasted_iota(i32, (w, 2 * w), 0)
        c = lane & (w - 1)
        delta = jnp.clip(kc - c + (_WIN_C - 1), 0, n_dc - 1)
        cstart = jnp.clip(c - _WIN_C // 2, 0, w - _WIN_C)
        valid = (kc >= cstart) & (kc < cstart + _WIN_C)
        for hh in range(_ATTN_HEADS):
            base = (pl.program_id(0) * _ATTN_HEADS + hh) * (n_dr * n_dc)

            def dr_body(dr, carry, hh=hh, base=base):
                acc = jnp.zeros((w, 2 * w), f32)
                for d in range(n_dc):
                    acc = jnp.where(delta == d, rpb_ref[base + dr * n_dc + d], acc)
                t_scr[hh, dr] = jnp.where(valid, acc, _NEG_INF)
                return carry

            lax.fori_loop(0, n_dr, dr_body, 0)
            t_scr[hh, masked] = jnp.full((w, 2 * w), _NEG_INF, f32)

    def vt_body(i, carry):
        for u in range(_ATTN_VT_UNROLL):
            blk_i = i * _ATTN_VT_UNROLL + u
            blk = v_ref[pl.ds(pl.multiple_of(blk_i * 2 * w, 2 * w), 2 * w), :].astype(f32)
            for hh in range(_ATTN_HEADS):
                vt_scr[hh, blk_i] = blk[:, hh * dh:(hh + 1) * dh].T.astype(bf16)
        return carry

    lax.fori_loop(0, rows // (2 * _ATTN_VT_UNROLL), vt_body, 0)
    kcx = kc_ref[...]
    vcx = vc_ref[...]
    vct = [vcx[:, hh * dh:(hh + 1) * dh].astype(f32).T.astype(bf16) for hh in range(_ATTN_HEADS)]

    def block_diag(q):
        zero = jnp.zeros((q.shape[0], dh), q.dtype)
        blocks = []
        for hh in range(_ATTN_HEADS):
            cols = [q[:, hh * dh:(hh + 1) * dh] if k == hh else zero for k in range(_ATTN_HEADS)]
            blocks.append(jnp.concatenate(cols, axis=1))
        return jnp.concatenate(blocks, axis=0)

    def pair_scores(i):
        r = 2 * i
        band = jnp.minimum(jnp.clip(r - _WIN_R // 2, 0, rows - _WIN_R - 1) & -2,
                           rows - _ATTN_KEY_ROWS)
        q0 = pl.multiple_of(r * w, 2 * w)
        k0 = pl.multiple_of(band * w, 2 * w)
        qbd = block_diag(q_ref[pl.ds(q0, 2 * w), :])
        kw = k_ref[pl.ds(k0, _ATTN_KEY_ROWS * w), :]
        drs = []
        for u in range(2):
            rs = jnp.clip(r + u - _WIN_R // 2, 0, rows - _WIN_R)
            drs.append((rs - (r + u) + (_WIN_R - 1), rs - band))
        blocks = [[] for _ in range(_ATTN_HEADS)]
        for kr in range(_ATTN_KEY_ROWS):
            idx = []
            for s0, off in drs:
                rel = kr - off
                idx.append(jnp.where((rel >= 0) & (rel < _WIN_R), s0 + rel, masked))
            for hh in range(_ATTN_HEADS):
                blocks[hh].append(jnp.where(first_row, t_scr[hh, idx[0]], t_scr[hh, idx[1]]))
        bias = jnp.concatenate([jnp.concatenate(blk, axis=0) for blk in blocks], axis=1)
        s_loc = lax.dot_general(kw, qbd, _NT_DIMS, preferred_element_type=f32) * scale + bias
        s_ctx = lax.dot_general(kcx, qbd, _NT_DIMS, preferred_element_type=f32) * scale
        return s_loc, s_ctx, band, q0

    def pair_output(s_loc, s_ctx, band, q0):
        mx = jnp.maximum(jnp.max(s_loc, axis=0, keepdims=True),
                         jnp.max(s_ctx, axis=0, keepdims=True))
        p_loc = jnp.exp(s_loc - mx)
        p_ctx = jnp.exp(s_ctx - mx)
        den = jnp.sum(p_loc, axis=0, keepdims=True) + jnp.sum(p_ctx, axis=0, keepdims=True)
        p_loc = p_loc.astype(bf16)
        p_ctx = p_ctx.astype(bf16)
        vb = band // 2
        outs = []
        for hh in range(_ATTN_HEADS):
            qs = slice(hh * 2 * w, (hh + 1) * 2 * w)
            vwt = jnp.concatenate([vt_scr[hh, vb + t] for t in range(_ATTN_KEY_ROWS // 2)], axis=1)
            o_t = (jnp.dot(vwt, p_loc[:, qs], preferred_element_type=f32)
                   + jnp.dot(vct[hh], p_ctx[:, qs], preferred_element_type=f32))
            outs.append((o_t / den[:, qs]).T)
        o_ref[pl.ds(q0, 2 * w), :] = jnp.concatenate(outs, axis=1).astype(o_ref.dtype)

    def pair_body(i, carry):
        scores = [pair_scores(i * _ATTN_PAIR_UNROLL + u) for u in range(_ATTN_PAIR_UNROLL)]
        for sc in scores:
            pair_output(*sc)
        return carry

    lax.fori_loop(0, rows // (2 * _ATTN_PAIR_UNROLL), pair_body, 0)

    for hh in range(_ATTN_HEADS):
        hs = slice(hh * dh, (hh + 1) * dh)
        s = lax.dot_general(qc_ref[:, hs], kcx[:, hs], _NT_DIMS, preferred_element_type=f32) * scale
        p = jnp.exp(s - jnp.max(s, axis=1, keepdims=True))
        den = jnp.sum(p, axis=1, keepdims=True)
        oc = jnp.dot(p.astype(bf16), vcx[:, hs], preferred_element_type=f32)
        oc_ref[:, hs] = (oc / den).astype(oc_ref.dtype)


def _attention(qkv_lat, qkv_ctx, rpb, *, batch, seq, ctx_len, n_heads):
    d = qkv_lat.shape[1] // 3
    dh = d // n_heads
    rows = seq // _GRID_W
    assert dh == _LANES and _GRID_W * 2 == _LANES and rows >= _WIN_R
    assert rows % (2 * _ATTN_PAIR_UNROLL) == 0 and rows >= _ATTN_KEY_ROWS
    assert rows % (2 * _ATTN_VT_UNROLL) == 0 and n_heads % _ATTN_HEADS == 0
    n_dr = 2 * _WIN_R - 1
    groups = n_heads // _ATTN_HEADS
    dg = _ATTN_HEADS * dh
    return pl.pallas_call(
        functools.partial(_attn_body, rows=rows, dh=dh),
        grid=(groups, batch),
        in_specs=[
            pl.BlockSpec(memory_space=pltpu.SMEM),
            pl.BlockSpec((seq, dg), lambda h, b: (b, h)),
            pl.BlockSpec((seq, dg), lambda h, b: (b, groups + h)),
            pl.BlockSpec((seq, dg), lambda h, b: (b, 2 * groups + h)),
            pl.BlockSpec((ctx_len, dg), lambda h, b: (b, h)),
            pl.BlockSpec((ctx_len, dg), lambda h, b: (b, groups + h)),
            pl.BlockSpec((ctx_len, dg), lambda h, b: (b, 2 * groups + h)),
        ],
        out_specs=[
            pl.BlockSpec((seq, dg), lambda h, b: (b, h)),
            pl.BlockSpec((ctx_len, dg), lambda h, b: (b, h)),
        ],
        out_shape=[
            jax.ShapeDtypeStruct((batch * seq, d), bf16),
            jax.ShapeDtypeStruct((batch * ctx_len, d), bf16),
        ],
        scratch_shapes=[
            pltpu.VMEM((_ATTN_HEADS, n_dr + 1, _GRID_W, 2 * _GRID_W), f32),
            pltpu.VMEM((_ATTN_HEADS, rows // 2, dh, 2 * _GRID_W), bf16),
        ],
        compiler_params=_cparams(("arbitrary", "arbitrary")),
        name="nbr_attention",
    )(rpb.reshape(-1), qkv_lat, qkv_lat, qkv_lat, qkv_ctx, qkv_ctx, qkv_ctx)


def _layer_norm(y, g, b):
    mu = jnp.mean(y, axis=-1, keepdims=True)
    yc = y - mu
    var = jnp.mean(yc * yc, axis=-1, keepdims=True)
    return yc * lax.rsqrt(var + _LN_EPS) * g + b


def _post_body(*refs, glu, alpha, tiles_per_row, row0):
    if glu:
        a_ref, wv_ref, wg_ref = refs[:3]
        rest = refs[3:]
    else:
        a_ref, wv_ref = refs[:2]
        rest = refs[2:]
    x_ref, g1_ref, sc2_ref, sh2_ref, lng_ref, lnb_ref, wr_ref, x1_ref, h2_ref, lg_ref = rest
    a = a_ref[...]
    o = jnp.dot(a, wv_ref[...], preferred_element_type=f32)
    if glu:
        o = o * jax.nn.sigmoid(jnp.dot(a, wg_ref[...], preferred_element_type=f32))
    r = row0 + pl.program_id(0) // tiles_per_row
    g1 = g1_ref[pl.ds(r, 1), :]
    x1 = _layer_norm(alpha * x_ref[...] + g1 * o, lng_ref[...], lnb_ref[...])
    h2 = x1 * (1.0 + sc2_ref[pl.ds(r, 1), :]) + sh2_ref[pl.ds(r, 1), :]
    x1_ref[...] = x1
    _store_token_tiles(h2_ref, _pack_bf16_pairs(h2))
    n_exp = wr_ref.shape[0]
    wr = wr_ref[...]
    w_hi = wr.astype(bf16)
    w_lo = (wr - w_hi.astype(f32)).astype(bf16)
    h_hi = h2.astype(bf16)
    h_lo = (h2 - h_hi.astype(f32)).astype(bf16)
    both = lax.dot_general(jnp.concatenate([w_hi, w_lo], axis=0), h_hi, _NT_DIMS,
                           preferred_element_type=f32)
    lg_ref[...] = (both[:n_exp] + both[n_exp:]
                   + lax.dot_general(w_hi, h_lo, _NT_DIMS, preferred_element_type=f32))


def _mixer_post(a_bf, weights_bf, x, m, layer, ln_g, ln_b, w_router_t, *, alpha, rows_per_mod,
                row0):
    mtot, d = x.shape
    n_exp = w_router_t.shape[0]
    glu = len(weights_bf) == 2
    tm = min(256 if glu else 512, mtot, rows_per_mod)
    tile_rows = d // 2 // _LANES
    const = lambda i: (0, 0)
    in_specs = [pl.BlockSpec((tm, d), lambda i: (i, 0))]
    in_specs += [pl.BlockSpec((d, d), const, pipeline_mode=pl.Buffered(1)) for _ in weights_bf]
    in_specs += [
        pl.BlockSpec((tm, d), lambda i: (i, 0)),
        _mod_spec(layer, 2, d, 1),
        _mod_spec(layer, 4, d, 1),
        _mod_spec(layer, 3, d, 1),
        pl.BlockSpec((1, d), const),
        pl.BlockSpec((1, d), const),
        pl.BlockSpec((n_exp, d), const),
    ]
    return pl.pallas_call(
        functools.partial(_post_body, glu=glu, alpha=alpha, tiles_per_row=rows_per_mod // tm,
                          row0=row0),
        grid=(mtot // tm,),
        in_specs=in_specs,
        out_specs=[
            pl.BlockSpec((tm, d), lambda i: (i, 0)),
            pl.BlockSpec((tm * tile_rows, _LANES), lambda i: (i, 0)),
            pl.BlockSpec((n_exp, tm), lambda i: (0, i)),
        ],
        out_shape=[
            jax.ShapeDtypeStruct((mtot, d), f32),
            jax.ShapeDtypeStruct((mtot * tile_rows, _LANES), jnp.uint32),
            jax.ShapeDtypeStruct((n_exp, mtot), f32),
        ],
        compiler_params=_cparams(("arbitrary",)),
        name="mixer_post_glu" if glu else "mixer_post",
    )(a_bf, *weights_bf, x, m, m, m, ln_g.reshape(1, d), ln_b.reshape(1, d), w_router_t)


def _cumsum_excl(mask01):
    rows, n = mask01.shape
    r = lax.broadcasted_iota(i32, (_LANES, _LANES), 0)
    c = lax.broadcasted_iota(i32, (_LANES, _LANES), 1)
    tri = jnp.where(r < c, 1.0, 0.0).astype(bf16)
    carry = jnp.zeros((rows, 1), f32)
    outs = []
    for t in range(n // _LANES):
        blk = mask01[:, t * _LANES:(t + 1) * _LANES]
        outs.append(jnp.dot(blk.astype(bf16), tri, preferred_element_type=f32) + carry)
        carry = carry + jnp.sum(blk, axis=1, keepdims=True)
    return outs[0] if len(outs) == 1 else jnp.concatenate(outs, axis=1)


def _route_body(lg_ref, idx_ref, gate_ref, posm_scr, aff_scr, *, n, cap, chunk):
    n_exp = lg_ref.shape[0]
    x = lg_ref[...]
    ex = jnp.exp(x - jnp.max(x, axis=0, keepdims=True))
    aff = ex / jnp.sum(ex, axis=0, keepdims=True)
    bits = pltpu.bitcast(aff, i32)

    thr = jnp.zeros((n_exp, 1), i32)
    for bit in range(30, -1, -1):
        cand = thr | (1 << bit)
        cnt = jnp.sum(jnp.where(bits >= cand, 1.0, 0.0), axis=1, keepdims=True)
        thr = jnp.where(cnt >= cap, cand, thr)
    gt = bits > thr
    eq = bits == thr
    need = cap - jnp.sum(jnp.where(gt, 1.0, 0.0), axis=1, keepdims=True)
    rank_eq = _cumsum_excl(jnp.where(eq, 1.0, 0.0))
    sel = gt | (eq & (rank_eq < need))
    pos = _cumsum_excl(jnp.where(sel, 1.0, 0.0))
    posm_scr[...] = jnp.where(sel, pos, -1.0)
    a_hi = aff.astype(bf16)
    r1 = aff - a_hi.astype(f32)
    a_mid = r1.astype(bf16)
    a_lo = (r1 - a_mid.astype(f32)).astype(bf16)
    for k, piece in enumerate((a_hi, a_mid, a_lo)):
        aff_scr[k] = piece.astype(f32)
    tok = lax.broadcasted_iota(i32, (1, n), 1)
    digits = jnp.concatenate([(tok >> 6).astype(f32), (tok & 63).astype(f32)], axis=0)
    pad = jnp.zeros((_SUBLANES - 5, n), f32)
    tok_base = pl.program_id(0) * n

    def per_expert(e, carry):
        prow = posm_scr[pl.ds(e, 1), :]
        rows = [digits] + [aff_scr[k, pl.ds(e, 1), :] for k in range(3)] + [pad]
        vals = jnp.concatenate(rows, axis=0).astype(bf16)
        outs = []
        for pc in range(cap // chunk):
            slot = (lax.broadcasted_iota(i32, (chunk, n), 0) + pc * chunk).astype(f32)
            hit = jnp.where(prow == slot, 1.0, 0.0).astype(bf16)
            outs.append(lax.dot_general(vals, hit, _NT_DIMS, preferred_element_type=f32))
        got = outs[0] if len(outs) == 1 else jnp.concatenate(outs, axis=1)
        idx = got[0:1] * 64.0 + got[1:2]
        idx_ref[pl.ds(e, 1), :] = idx.astype(i32) + tok_base
        gate_ref[pl.ds(e, 1), :] = got[2:3] + got[3:4] + got[4:5]
        return carry

    lax.fori_loop(0, n_exp, per_expert, 0)


def _route(logits_t, *, batch, n):
    n_exp = logits_t.shape[0]
    cap = _CAPACITY_FACTOR * n // n_exp
    chunk = min(_LANES, cap)
    assert n % _LANES == 0 and cap % chunk == 0 and n <= 64 * 64
    return pl.pallas_call(
        functools.partial(_route_body, n=n, cap=cap, chunk=chunk),
        grid=(batch,),
        in_specs=[pl.BlockSpec((n_exp, n), lambda b: (0, b))],
        out_specs=[
            pl.BlockSpec((None, n_exp, cap), lambda b: (b, 0, 0)),
            pl.BlockSpec((None, n_exp, cap), lambda b: (b, 0, 0)),
        ],
        out_shape=[
            jax.ShapeDtypeStruct((batch, n_exp, cap), i32),
            jax.ShapeDtypeStruct((batch, n_exp, cap), f32),
        ],
        scratch_shapes=[pltpu.VMEM((n_exp, n), f32), pltpu.VMEM((3, n_exp, n), f32)],
        compiler_params=_cparams(("arbitrary",)),
        name="route_topk",
    )(logits_t)


def _pack_bf16_pairs(h):
    half = h.shape[1] // 2
    u = pltpu.bitcast(h.astype(bf16).astype(f32), jnp.uint32)
    return (u[:, :half] & jnp.uint32(0xFFFF0000)) | (u[:, half:] >> 16)


def _store_token_tiles(ref, packed):
    m, width = packed.shape
    r = width // _LANES
    for s in range(r):
        ref[pl.ds(s, m, stride=r), :] = packed[:, s * _LANES:(s + 1) * _LANES]


def _load_token_tiles(ref, m):
    r = ref.shape[0] // m
    chunks = [ref[pl.ds(s, m, stride=r), :] for s in range(r)]
    return jnp.concatenate(chunks, axis=1)


def _unpack_bf16_pairs(p):
    hi = pltpu.bitcast(p & jnp.uint32(0xFFFF0000), f32).astype(bf16)
    lo = pltpu.bitcast(p << 16, f32).astype(bf16)
    return jnp.concatenate([hi, lo], axis=1)


_GATHER_UNROLL = 8


def _moe_body(idx_ref, *refs, seg_counts, m_slots, n_tiles):
    n_src = len(seg_counts)
    srcs = refs[:n_src]
    wg_ref, wu_ref, wd_ref, o_ref, xp_scr, xe16, hmid, sem = refs[n_src:]
    e = pl.program_id(0)
    j = pl.program_id(1)
    tf = wg_ref.shape[1]

    tile_rows = xp_scr.shape[0] // m_slots

    def tile_copy(src, row, slot):
        r0 = pl.multiple_of(row * tile_rows, tile_rows)
        s0 = pl.multiple_of(slot * tile_rows, tile_rows)
        return pltpu.make_async_copy(src.at[pl.ds(r0, tile_rows), :],
                                     xp_scr.at[pl.ds(s0, tile_rows), :], sem)

    def for_each_slot(fn):
        slot0 = 0
        for src, cnt in zip(srcs, seg_counts):
            def body(i, c, src=src, slot0=slot0):
                for u in range(_GATHER_UNROLL):
                    fn(src, slot0 + i * _GATHER_UNROLL + u)
                return c

            lax.fori_loop(0, cnt // _GATHER_UNROLL, body, 0)
            slot0 += cnt

    def issue_gather(expert):
        base = expert * m_slots
        for_each_slot(lambda src, slot: tile_copy(src, idx_ref[base + slot], slot).start())

    n_exp = pl.num_programs(0)
    last_step = 2 * n_tiles - 1

    @pl.when(j == 0)
    def _rows():
        @pl.when(e == 0)
        def _():
            issue_gather(0)

        for_each_slot(lambda src, slot: tile_copy(src, 0, slot).wait())
        xe16[...] = _unpack_bf16_pairs(_load_token_tiles(xp_scr, m_slots))

    @pl.when(j < n_tiles)
    def _up():
        x = xe16[...]
        a = jnp.dot(x, wg_ref[...].astype(bf16), preferred_element_type=f32)
        u = jnp.dot(x, wu_ref[...].astype(bf16), preferred_element_type=f32)
        hmid[j] = (a * jax.nn.sigmoid(a) * u).astype(bf16)

    slot_src = [src for src, cnt in zip(srcs, seg_counts) for _ in range(cnt)]
    per_step = m_slots // n_tiles
    next_base = jnp.minimum(e + 1, n_exp - 1) * m_slots
    for t in range(n_tiles):
        @pl.when(j == n_tiles + t)
        def _down(t=t):
            for slot in range(t * per_step, (t + 1) * per_step):
                tile_copy(slot_src[slot], idx_ref[next_base + slot], slot).start()
            y = jnp.dot(hmid[0], wd_ref[0:tf, :].astype(bf16), preferred_element_type=f32)
            for k in range(1, n_tiles):
                y += jnp.dot(hmid[k], wd_ref[k * tf:(k + 1) * tf, :].astype(bf16),
                             preferred_element_type=f32)
            o_ref[...] = y

    @pl.when((e == n_exp - 1) & (j == last_step))
    def _drain():
        for_each_slot(lambda src, slot: tile_copy(src, 0, slot).wait())


def _moe_experts(idx_flat, srcs, seg_counts, w_gate, w_up, w_down, layer):
    _, n_exp, d, ff = w_gate.shape
    m_slots = sum(seg_counts)
    tf = min(512, ff)
    n_tiles = ff // tf
    assert d // tf == n_tiles and all(c % _GATHER_UNROLL == 0 for c in seg_counts)
    assert m_slots % n_tiles == 0
    n_src = len(srcs)
    up_idx = lambda e, j, idx: (layer, e, 0, jnp.minimum(j, n_tiles - 1))
    dn_idx = lambda e, j, idx: (layer, e, 0, jnp.maximum(j - n_tiles, 0))
    grid_spec = pltpu.PrefetchScalarGridSpec(
        num_scalar_prefetch=1,
        grid=(n_exp, 2 * n_tiles),
        in_specs=[pl.BlockSpec(memory_space=pl.ANY)] * n_src + [
            pl.BlockSpec((None, None, d, tf), up_idx),
            pl.BlockSpec((None, None, d, tf), up_idx),
            pl.BlockSpec((None, None, ff, tf), dn_idx),
        ],
        out_specs=pl.BlockSpec((None, m_slots, tf),
                               lambda e, j, idx: (e, 0, jnp.maximum(j - n_tiles, 0))),
        scratch_shapes=[
            pltpu.VMEM((m_slots * (d // 2 // _LANES), _LANES), jnp.uint32),
            pltpu.VMEM((m_slots, d), bf16),
            pltpu.VMEM((n_tiles, m_slots, tf), bf16),
            pltpu.SemaphoreType.DMA(()),
        ],
    )
    return pl.pallas_call(
        functools.partial(_moe_body, seg_counts=tuple(seg_counts), m_slots=m_slots,
                          n_tiles=n_tiles),
        grid_spec=grid_spec,
        out_shape=jax.ShapeDtypeStruct((n_exp, m_slots, d), f32),
        compiler_params=_cparams(("arbitrary", "arbitrary")),
        name="moe_experts",
    )(idx_flat, *srcs, w_gate, w_up, w_down)


_SCATTER_UNROLL = 2


_POST_ROWS = 256


def _combine_body(idx_ref, gate_ref, ye_ref, g2_ref, lng_ref, lnb_ref, x1_hbm, o_hbm, acc, xbuf,
                  obuf, sem_in, sem_out, *, m_slots, slot_base, cap, n, alpha, row0, row_stride):
    b = pl.program_id(0)
    e = pl.program_id(1)

    @pl.when(e == 0)
    def _():
        acc[...] = jnp.zeros_like(acc)

    base = e * m_slots + slot_base + b * cap
    tok0 = b * n

    def body(i, carry):
        j0 = i * _SCATTER_UNROLL
        toks = [idx_ref[base + j0 + u] - tok0 for u in range(_SCATTER_UNROLL)]
        vals = [acc[pl.ds(toks[u], 1), :] + gate_ref[base + j0 + u] * ye_ref[pl.ds(j0 + u, 1), :]
                for u in range(_SCATTER_UNROLL)]
        for u in range(_SCATTER_UNROLL):
            acc[pl.ds(toks[u], 1), :] = vals[u]
        return carry

    lax.fori_loop(0, cap // _SCATTER_UNROLL, body, 0)

    @pl.when(e == pl.num_programs(1) - 1)
    def _post():
        tr = xbuf.shape[1]
        n_t = n // tr
        g2 = g2_ref[pl.ds(row0 + b * row_stride, 1), :]

        def rows(t):
            return pl.ds(pl.multiple_of(b * n + t * tr, tr), tr)

        def x_copy(t):
            return pltpu.make_async_copy(x1_hbm.at[rows(t), :], xbuf.at[t % 2], sem_in.at[t % 2])

        def o_copy(t):
            return pltpu.make_async_copy(obuf.at[t % 2], o_hbm.at[rows(t), :], sem_out.at[t % 2])

        x_copy(0).start()
        for t in range(n_t):
            if t + 1 < n_t:
                x_copy(t + 1).start()
            x_copy(t).wait()
            if t >= 2:
                o_copy(t - 2).wait()
            f = acc[t * tr:(t + 1) * tr, :]
            obuf[t % 2] = _layer_norm(alpha * xbuf[t % 2] + g2 * f, lng_ref[...], lnb_ref[...])
            o_copy(t).start()
        for t in range(max(n_t - 2, 0), n_t):
            o_copy(t).wait()


def _combine_post(idx_flat, gate_flat, ye, x1, m, layer, ln_g, ln_b, *, batch, n, cap, slot_base,
                  alpha, row0, row_stride):
    n_exp, m_slots, d = ye.shape
    assert slot_base % cap == 0 and cap % _SCATTER_UNROLL == 0
    blk0 = slot_base // cap
    tr = min(_POST_ROWS, n)
    const = lambda b, e, idx, gate: (0, 0)
    grid_spec = pltpu.PrefetchScalarGridSpec(
        num_scalar_prefetch=2,
        grid=(batch, n_exp),
        in_specs=[
            pl.BlockSpec((None, cap, d), lambda b, e, idx, gate: (e, blk0 + b, 0)),
            pl.BlockSpec((None, _SUBLANES, d), lambda b, e, idx, gate: (layer, 0, 5)),
            pl.BlockSpec((1, d), const),
            pl.BlockSpec((1, d), const),
            pl.BlockSpec(memory_space=pl.ANY),
        ],
        out_specs=pl.BlockSpec(memory_space=pl.ANY),
        scratch_shapes=[
            pltpu.VMEM((n, d), f32),
            pltpu.VMEM((2, tr, d), f32),
            pltpu.VMEM((2, tr, d), f32),
            pltpu.SemaphoreType.DMA((2,)),
            pltpu.SemaphoreType.DMA((2,)),
        ],
    )
    return pl.pallas_call(
        functools.partial(_combine_body, m_slots=m_slots, slot_base=slot_base, cap=cap, n=n,
                          alpha=alpha, row0=row0, row_stride=row_stride),
        grid_spec=grid_spec,
        out_shape=jax.ShapeDtypeStruct((batch * n, d), f32),
        compiler_params=_cparams(("arbitrary", "arbitrary")),
        name="moe_combine",
    )(idx_flat, gate_flat, ye, m, ln_g.reshape(1, d), ln_b.reshape(1, d), x1)


def _disc_body(lr_ref, li_ref, ls_ref, bre_ref, bim_ref, ar_ref, ai_ref, br_ref, bi_ref):
    lr = jnp.minimum(lr_ref[...], -1e-4)
    li = li_ref[...]
    dt = jnp.exp(ls_ref[...])
    mag = jnp.exp(lr * dt)
    ar = mag * jnp.cos(li * dt)
    ai = mag * jnp.sin(li * dt)
    nr = ar - 1.0
    den = lr * lr + li * li
    cr = (nr * lr + ai * li) / den
    ci = (ai * lr - nr * li) / den
    ar_ref[...] = ar
    ai_ref[...] = ai
    br_ref[...] = cr * bre_ref[...] - ci * bim_ref[...]
    bi_ref[...] = cr * bim_ref[...] + ci * bre_ref[...]


def _s5_discretize(lam_re, lam_im, log_step, b_re, b_im):
    shape = b_re.shape
    flat = (shape[0] * shape[1], shape[2] * shape[3])
    bc = lambda a: jnp.broadcast_to(a, shape).reshape(flat)
    outs = pl.pallas_call(
        _disc_body,
        out_shape=[jax.ShapeDtypeStruct(flat, f32)] * 4,
        name="s5_discretize",
    )(bc(lam_re[..., None]), bc(lam_im[..., None]), bc(log_step[:, :, None, None]),
      b_re.reshape(flat), b_im.reshape(flat))
    ar, ai, br, bi = [o.reshape(shape) for o in outs]
    return ar[..., 0], ai[..., 0], br, bi


_SCAN_UNROLL = 8


def _cmul(ar, ai, xr, xi):
    return ar * xr - ai * xi, ar * xi + ai * xr


def _cpow(ar, ai, k):
    rr, ri = jnp.ones_like(ar), jnp.zeros_like(ai)
    while k:
        if k & 1:
            rr, ri = _cmul(ar, ai, rr, ri)
        ar, ai = _cmul(ar, ai, ar, ai)
        k >>= 1
    return rr, ri


def _segment_scan(bu_ref, seg_len, s_dim, ar, ai, init, reverse, xs_ref=None):
    ns = _SUBLANES
    unroll = math.gcd(seg_len, _SCAN_UNROLL)
    assert xs_ref is None or unroll % 2 == 0

    def outer(jo, carry):
        xr, xi = carry
        prev = None
        for ji in range(unroll):
            jj = jo * unroll + ji
            j = (seg_len - 1 - jj) if reverse else jj
            r0 = pl.multiple_of(j * ns, ns)
            br = bu_ref[pl.ds(r0, ns), 0:s_dim]
            bi = bu_ref[pl.ds(r0, ns), s_dim:2 * s_dim]
            xr, xi = ar * xr - ai * xi + br, ar * xi + ai * xr + bi
            if xs_ref is not None:
                if ji % 2 == 0:
                    prev = (xr, xi)
                else:
                    pair = ((xr, prev[0]), (xi, prev[1])) if reverse else ((prev[0], xr),
                                                                          (prev[1], xi))
                    p0 = pl.multiple_of((j if reverse else j - 1) * ns, 2 * ns)
                    for half, (lo, hi) in enumerate(pair):
                        xs_ref[pl.ds(p0, 2 * ns), half * s_dim:(half + 1) * s_dim] = (
                            jnp.concatenate([lo, hi], axis=0).astype(bf16))
        return xr, xi

    return lax.fori_loop(0, seg_len // unroll, outer, init)


def _segment_inits(ends, a_len, h0, reverse):
    er, ei = ends
    alr, ali = a_len
    ns = _SUBLANES
    order = range(ns - 1, -1, -1) if reverse else range(ns)
    cr, ci = h0
    inits_r, inits_i = [None] * ns, [None] * ns
    for s in order:
        inits_r[s], inits_i[s] = cr, ci
        pr, pi = _cmul(alr, ali, cr, ci)
        cr, ci = pr + er[s:s + 1, :], pi + ei[s:s + 1, :]
    return (jnp.concatenate(inits_r, axis=0), jnp.concatenate(inits_i, axis=0)), (cr, ci)


def _s5_body(x_ref, xc_ref, sh_ref, sc_ref, a_ref, wb_ref, cm_ref, d_ref, z_ref,
             up_scr, ucp_scr, bu_scr, xs_scr, buc_scr, y_scr, yn_scr, *, n, nc, batch):
    ns = _SUBLANES
    seg, segc = n // ns, nc // ns
    s_dim = a_ref.shape[-1]
    b = pl.program_id(0)
    sc_l, sh_l = sc_ref[pl.ds(b, 1), :], sh_ref[pl.ds(b, 1), :]
    sc_c, sh_c = sc_ref[batch:batch + 1, :], sh_ref[batch:batch + 1, :]

    for j in range(seg):
        up_scr[j * ns:(j + 1) * ns, :] = x_ref[pl.ds(j, ns, stride=seg), :] * (1.0 + sc_l) + sh_l
    for j in range(segc):
        ucp_scr[j * ns:(j + 1) * ns, :] = xc_ref[pl.ds(j, ns, stride=segc), :] * (1.0 + sc_c) + sh_c

    u = up_scr[...]
    y_scr[...] = d_ref[...] * u
    uc_bf = ucp_scr[...].astype(bf16)
    zero = (jnp.zeros((ns, s_dim), f32), jnp.zeros((ns, s_dim), f32))
    u_bf = u.astype(bf16)
    for direction in range(2):
        reverse = direction == 1
        ar1, ai1 = a_ref[direction, 0:1, :], a_ref[direction, 1:2, :]
        ar = jnp.broadcast_to(ar1, (ns, s_dim))
        ai = jnp.broadcast_to(ai1, (ns, s_dim))
        buc_scr[...] = jnp.dot(uc_bf, wb_ref[direction], preferred_element_type=f32)
        ends_c = _segment_scan(buc_scr, segc, s_dim, ar, ai, zero, reverse)
        zero1 = (jnp.zeros((1, s_dim), f32), jnp.zeros((1, s_dim), f32))
        _, h0 = _segment_inits(ends_c, _cpow(ar1, ai1, segc), zero1, reverse)
        for lo in (0, n // 2):
            bu_scr[lo:lo + n // 2, :] = jnp.dot(u_bf[lo:lo + n // 2], wb_ref[direction],
                                                preferred_element_type=f32)
        ends = _segment_scan(bu_scr, seg, s_dim, ar, ai, zero, reverse)
        inits, _ = _segment_inits(ends, _cpow(ar1, ai1, seg), h0, reverse)
        _segment_scan(bu_scr, seg, s_dim, ar, ai, inits, reverse, xs_ref=xs_scr)
        for lo in (0, n // 2):
            y_scr[lo:lo + n // 2, :] += jnp.dot(xs_scr[lo:lo + n // 2, :], cm_ref[direction],
                                                preferred_element_type=f32)

    for j in range(seg):
        yn_scr[pl.ds(j, ns, stride=seg), :] = y_scr[j * ns:(j + 1) * ns, :]
    z_ref[...] = jax.nn.gelu(yn_scr[...]).astype(z_ref.dtype)


def _s5_core(x_lat, x_ctx, m, layer, a_blk, wb_blk, cm_blk, d_skip, *, batch, n, nc):
    d = x_lat.shape[1]
    n_blk = d // _LANES
    s_dim = a_blk.shape[-1]
    assert n % (_SUBLANES * _SCAN_UNROLL) == 0 and nc % _SUBLANES == 0
    return pl.pallas_call(
        functools.partial(_s5_body, n=n, nc=nc, batch=batch),
        grid=(batch, n_blk),
        in_specs=[
            pl.BlockSpec((n, _LANES), lambda b, g: (b, g)),
            pl.BlockSpec((nc, _LANES), lambda b, g: (b, g)),
            pl.BlockSpec((None, _SUBLANES, _LANES), lambda b, g: (layer, 0, g)),
            pl.BlockSpec((None, _SUBLANES, _LANES), lambda b, g: (layer, 0, n_blk + g)),
            pl.BlockSpec((2, None, 2, s_dim), lambda b, g: (0, g, 0, 0)),
            pl.BlockSpec((2, None, _LANES, 2 * s_dim), lambda b, g: (0, g, 0, 0)),
            pl.BlockSpec((2, None, 2 * s_dim, _LANES), lambda b, g: (0, g, 0, 0)),
            pl.BlockSpec((1, _LANES), lambda b, g: (0, g)),
        ],
        out_specs=pl.BlockSpec((n, _LANES), lambda b, g: (b, g)),
        out_shape=jax.ShapeDtypeStruct((batch * n, d), bf16),
        scratch_shapes=[
            pltpu.VMEM((n, _LANES), f32),
            pltpu.VMEM((nc, _LANES), f32),
            pltpu.VMEM((n, 2 * s_dim), f32),
            pltpu.VMEM((n, 2 * s_dim), bf16),
            pltpu.VMEM((nc, 2 * s_dim), f32),
            pltpu.VMEM((n, _LANES), f32),
            pltpu.VMEM((n, _LANES), f32),
        ],
        compiler_params=_cparams(("arbitrary", "arbitrary")),
        name="s5_core",
    )(x_lat, x_ctx, m, m, a_blk, wb_blk, cm_blk, d_skip.reshape(1, d))


def _s5_block_params(ar, ai, br, bi, c_re, c_im):
    _, n_groups, p_dim, gh = br.shape
    gpb = _LANES // gh
    n_blk = n_groups // gpb
    s_dim = gpb * p_dim
    eye = jnp.eye(gpb, dtype=f32)
    a_blk = jnp.stack([ar.reshape(2, n_blk, s_dim), ai.reshape(2, n_blk, s_dim)], axis=2)

    def b_mat(bx):
        t = bx.reshape(2, n_blk, gpb, p_dim, gh)
        return jnp.einsum("dngph,gk->dnghkp", t, eye).reshape(2, n_blk, gpb * gh, s_dim)

    def c_mat(cx):
        t = cx.reshape(2, n_blk, gpb, gh, p_dim)
        return jnp.einsum("dnghp,gk->dngpkh", t, eye).reshape(2, n_blk, s_dim, gpb * gh)

    wb = jnp.concatenate([b_mat(br), b_mat(bi)], axis=-1).astype(bf16)
    cm = jnp.concatenate([c_mat(c_re), -c_mat(c_im)], axis=-2).astype(bf16)
    return a_blk, wb, cm


def _moe_layer(h2_list, logits_list, x1_list, dims, mod_rows, w_gate, w_up, w_down, m, layer,
               ln_g, ln_b, *, batch, alpha):
    idxs, gates, caps = [], [], []
    for lg, n in zip(logits_list, dims):
        idx, gate = _route(lg, batch=batch, n=n)
        idxs.append(idx)
        gates.append(gate)
        caps.append(idx.shape[-1])
    n_exp = w_gate.shape[1]
    idx_all = jnp.concatenate([jnp.swapaxes(i, 0, 1).reshape(n_exp, -1) for i in idxs], axis=1)
    gate_all = jnp.concatenate([jnp.swapaxes(g, 0, 1).reshape(n_exp, -1) for g in gates], axis=1)
    seg_counts = [batch * c for c in caps]
    idx_flat = idx_all.reshape(-1)
    gate_flat = gate_all.reshape(-1)
    ye = _moe_experts(idx_flat, h2_list, seg_counts, w_gate, w_up, w_down, layer)
    outs, slot_base = [], 0
    for n, cap, x1, (row0, row_stride) in zip(dims, caps, x1_list, mod_rows):
        outs.append(_combine_post(idx_flat, gate_flat, ye, x1, m, layer, ln_g, ln_b, batch=batch,
                                  n=n, cap=cap, slot_base=slot_base, alpha=alpha, row0=row0,
                                  row_stride=row_stride))
        slot_base += batch * cap
    return outs


def _forward(x, c, ctx, c_ctx, w_mod, b_mod, ln_g, ln_b, na_w_qkv, na_w_o, na_rpb, s5_lam_re,
             s5_lam_im, s5_log_step, s5_b_re, s5_b_im, s5_c_re, s5_c_im, s5_d, s5_w_val, s5_w_gate,
             moe_w_router, moe_w_gate, moe_w_up, moe_w_down, *, n_heads):
    batch, seq, d = x.shape
    ctx_len = ctx.shape[1]
    depth = w_mod.shape[0]
    assert depth == 2
    alpha = (2 * depth) ** 0.25

    cond = jnp.concatenate([c, c_ctx[None, :]], axis=0)
    m = _modulation(cond, w_mod, b_mod)
    x_lat = x.reshape(batch * seq, d)
    x_ctx = ctx.reshape(batch * ctx_len, d)
    lat = dict(rows_per_mod=seq, row0=0)
    cx = dict(rows_per_mod=batch * ctx_len, row0=batch)

    w_qkv = na_w_qkv[0].astype(bf16)
    w_o = na_w_o[0].astype(bf16)
    qkv_lat = _mod_proj(x_lat, m, 0, w_qkv, **lat)
    qkv_ctx = _mod_proj(x_ctx, m, 0, w_qkv, **cx)
    o_lat, o_ctx = _attention(qkv_lat, qkv_ctx, na_rpb[0], batch=batch, seq=seq, ctx_len=ctx_len,
                              n_heads=n_heads)
    wr_t = moe_w_router[0].T
    x1_lat, h2_lat, lg_lat = _mixer_post(o_lat, [w_o], x_lat, m, 0, ln_g[0, 0], ln_b[0, 0], wr_t,
                                         alpha=alpha, **lat)
    x1_ctx, h2_ctx, lg_ctx = _mixer_post(o_ctx, [w_o], x_ctx, m, 0, ln_g[0, 0], ln_b[0, 0], wr_t,
                                         alpha=alpha, **cx)
    lat_rows, ctx_rows = (0, 1), (batch, 0)
    x_lat, x_ctx = _moe_layer([h2_lat, h2_ctx], [lg_lat, lg_ctx], [x1_lat, x1_ctx],
                              [seq, ctx_len], [lat_rows, ctx_rows], moe_w_gate, moe_w_up,
                              moe_w_down, m, 0, ln_g[0, 1], ln_b[0, 1], batch=batch, alpha=alpha)

    ar, ai, br, bi = _s5_discretize(s5_lam_re[0], s5_lam_im[0], s5_log_step[0], s5_b_re[0],
                                    s5_b_im[0])
    a_blk, wb_blk, cm_blk = _s5_block_params(ar, ai, br, bi, s5_c_re[0], s5_c_im[0])
    z = _s5_core(x_lat, x_ctx, m, 1, a_blk, wb_blk, cm_blk, s5_d[0], batch=batch, n=seq,
                 nc=ctx_len)
    x1_lat, h2_lat, lg_lat = _mixer_post(z, [s5_w_val[0].astype(bf16), s5_w_gate[0].astype(bf16)],
                                         x_lat, m, 1, ln_g[1, 0], ln_b[1, 0], moe_w_router[1].T,
                                         alpha=alpha, **lat)
    (x_lat,) = _moe_layer([h2_lat], [lg_lat], [x1_lat], [seq], [lat_rows], moe_w_gate, moe_w_up,
                          moe_w_down, m, 1, ln_g[1, 1], ln_b[1, 1], batch=batch, alpha=alpha)
    return x_lat.reshape(batch, seq, d)


def kernel(x, c, ctx, c_ctx, w_mod, b_mod, ln_g, ln_b, na_w_qkv, na_w_o, na_rpb, s5_lam_re,
           s5_lam_im, s5_log_step, s5_b_re, s5_b_im, s5_c_re, s5_c_im, s5_d, s5_w_val, s5_w_gate,
           moe_w_router, moe_w_gate, moe_w_up, moe_w_down):
    return _forward(x, c, ctx, c_ctx, w_mod, b_mod, ln_g, ln_b, na_w_qkv, na_w_o, na_rpb,
                    s5_lam_re, s5_lam_im, s5_log_step, s5_b_re, s5_b_im, s5_c_re, s5_c_im, s5_d,
                    s5_w_val, s5_w_gate, moe_w_router, moe_w_gate, moe_w_up, moe_w_down,
                    n_heads=_N_HEADS)
```
